```python
import math
import jax
import jax.numpy as jnp
from jax import lax
import numpy as np

D_MODEL = 1024
BATCH = 32
SEQ = 256
DEPTH = 4
DEC_BATCH = 2
DEC_SEQ = 4096
PAST_LEN = 512

GRID_W = 64
N_AB_LAYERS = (DEPTH + 1) // 2
N_CD_LAYERS = DEPTH // 2
MIX_WIDTH = D_MODEL
HEAD_DIM = 64
ROPE_THETA = 10000.0
RMS_EPS = 1e-6
Q_BLOCK = 128
CONV_W = 4
CONV_LEFT = 2
CONV_RIGHT = CONV_W - 1 - CONV_LEFT
LRU_WIDTH = MIX_WIDTH // 2
LRU_BLOCKS = 8
LRU_BW = LRU_WIDTH // LRU_BLOCKS
LRU_C = 8.0
DIFF_HEADS = (MIX_WIDTH // 2) // (2 * HEAD_DIM)
DIFF_QK = DIFF_HEADS * 2 * HEAD_DIM
DIFF_V = DIFF_HEADS * 2 * HEAD_DIM
IN_AB = 2 * LRU_WIDTH + 2 * DIFF_QK + DIFF_V
DELTA_HEADS = 4
DELTA_DK = (MIX_WIDTH // 2) // DELTA_HEADS
DELTA_DV = DELTA_DK
DELTA_WIDTH = DELTA_HEADS * DELTA_DV
DELTA_CONV_CH = 2 * DELTA_HEADS * DELTA_DK + DELTA_WIDTH
DELTA_CHUNK = 64
GQA_HEADS = (MIX_WIDTH // 2) // HEAD_DIM
GQA_KV_HEADS = 2
GQA_GROUP = GQA_HEADS // GQA_KV_HEADS
GQA_WIDTH = GQA_HEADS * HEAD_DIM
IN_CD = DELTA_CONV_CH + DELTA_WIDTH + 4 * DELTA_HEADS + GQA_WIDTH + 2 * GQA_KV_HEADS * HEAD_DIM
N_EXPERTS = 16
EC_CAPACITY_FACTOR = 2
EXPERT_FF = D_MODEL // 2

kernel_name = 'hybrid_diffusion_lru_diffattn_deltanet_gqa_ec'


def rms_norm(x, g):
    xf = x.astype(jnp.float32)
    y = xf * lax.rsqrt(jnp.mean(xf * xf, axis=-1, keepdims=True) + RMS_EPS)
    return (y * g.astype(jnp.float32)).astype(x.dtype)


def l2_normalize(x):
    return x * lax.rsqrt(jnp.sum(x * x, axis=-1, keepdims=True) + 1e-6)


def split_last(x, sizes):
    out, start = [], 0
    for s in sizes:
        out.append(x[..., start:start + s])
        start += s
    return out


def axial_rope_tables(n_tokens, head_dim):
    rows = n_tokens // GRID_W
    row = jnp.repeat(jnp.arange(rows, dtype=jnp.float32), GRID_W)
    col = jnp.tile(jnp.arange(GRID_W, dtype=jnp.float32), rows)
    quarter = head_dim // 4
    inv_freq = ROPE_THETA ** (-jnp.arange(quarter, dtype=jnp.float32) / quarter)
    ang = jnp.concatenate([row[:, None] * inv_freq, col[:, None] * inv_freq], axis=-1)
    return jnp.cos(ang), jnp.sin(ang)


def apply_rope(x, cos, sin):
    half = x.shape[-1] // 2
    shape = (1, cos.shape[0]) + (1,) * (x.ndim - 3) + (half,)
    c = cos.reshape(shape)
    s = sin.reshape(shape)
    x1, x2 = x[..., :half], x[..., half:]
    return jnp.concatenate([x1 * c - x2 * s, x2 * c + x1 * s], axis=-1)


def centred_depthwise_conv(x, w):
    T = x.shape[1]
    xp = jnp.pad(x, ((0, 0), (CONV_LEFT, CONV_RIGHT), (0, 0)))
    y = xp[:, 0:T] * w[0]
    for tap in range(1, CONV_W):
        y = y + xp[:, tap:tap + T] * w[tap]
    return y


def sweep_query_blocks(block_fn, q):
    B, T = q.shape[:2]
    qb_len = min(Q_BLOCK, T)
    nb = T // qb_len
    qb = jnp.moveaxis(q.reshape((B, nb, qb_len) + q.shape[2:]), 1, 0)
    out = jnp.moveaxis(lax.map(block_fn, qb), 0, 1)
    return out.reshape((B, T) + out.shape[3:])


def linear_recurrence(a, b, h0):
    def combine(left, right):
        return left[0] * right[0], right[0] * left[1] + right[1]
    a_cum, b_cum = lax.associative_scan(combine, (a, b), axis=1)
    return a_cum * h0[:, None, :] + b_cum


def rglru_coeffs(u, w_a, b_a, w_i, b_i, lam):
    B, T, W = u.shape
    ub = u.reshape(B, T, LRU_BLOCKS, LRU_BW)
    r = jax.nn.sigmoid(jnp.einsum('btnc,ncd->btnd', ub, w_a).reshape(B, T, W) + b_a)
    i = jax.nn.sigmoid(jnp.einsum('btnc,ncd->btnd', ub, w_i).reshape(B, T, W) + b_i)
    log_a = -LRU_C * r * jax.nn.softplus(-lam)
    a = jnp.exp(log_a)
    b = jnp.sqrt(-jnp.expm1(2.0 * log_a)) * (i * u)
    return a, b


def rglru_mixer(xa, ga, conv_w, conv_b, w_a, b_a, w_i, b_i, lam, h0):
    u = centred_depthwise_conv(xa, conv_w) + conv_b
    a_f, b_f = rglru_coeffs(u, w_a[0], b_a[0], w_i[0], b_i[0], lam[0])
    h_f = linear_recurrence(a_f, b_f, h0[:, 0])
    a_b, b_b = rglru_coeffs(u, w_a[1], b_a[1], w_i[1], b_i[1], lam[1])
    h_b = jnp.flip(linear_recurrence(jnp.flip(a_b, 1), jnp.flip(b_b, 1), h0[:, 1]), 1)
    y = (h_f + h_b) * jax.nn.gelu(ga)
    return y, jnp.stack([h_f[:, -1], h_b[:, 0]], axis=1)


def diff_attention_block(qb, k, v, lam):
    s = jnp.einsum('bqhcd,bkhcd->bhcqk', qb, k) * HEAD_DIM ** -0.5
    p = jax.nn.softmax(s, axis=-1)
    a = p[:, :, 0] - lam * p[:, :, 1]
    return jnp.einsum('bhqk,bkhe->bqhe', a, v)


def chunk_gated_delta_rule(q, k, v, beta, g, s0):
    B, T, H, dk = q.shape
    dv = v.shape[-1]
    C = DELTA_CHUNK
    n = T // C

    def chunks(x):
        x = x.reshape((B, n, C, H) + x.shape[3:])
        return jnp.moveaxis(jnp.swapaxes(x, 2, 3), 1, 0)

    q, k, v, beta, g = (chunks(t) for t in (q, k, v, beta, g))
    G = jnp.cumsum(g, axis=-1)
    incl = jnp.tril(jnp.ones((C, C), dtype=bool))
    strict = jnp.tril(jnp.ones((C, C), dtype=bool), -1)
    decay = jnp.exp(jnp.where(incl, G[..., :, None] - G[..., None, :], -jnp.inf))
    kb = k * beta[..., None]
    a = jnp.where(strict, jnp.einsum('nbhid,nbhjd->nbhij', kb, k) * decay, 0.0)
    m = a + jnp.eye(C, dtype=a.dtype)
    rhs = jnp.concatenate([kb * jnp.exp(G)[..., None], v * beta[..., None]], axis=-1)
    sol = lax.linalg.triangular_solve(m, rhs, left_side=True, lower=True, unit_diagonal=True)
    w, u = sol[..., :dk], sol[..., dk:]
    qk = jnp.where(incl, jnp.einsum('nbhid,nbhjd->nbhij', q, k) * decay, 0.0)
    q_dec = q * jnp.exp(G)[..., None]
    k_tail = k * jnp.exp(G[..., -1:] - G)[..., None]
    chunk_decay = jnp.exp(G[..., -1])

    def step(s, xs):
        w_c, u_c, qk_c, qd_c, kt_c, cd_c = xs
        v_new = u_c - jnp.einsum('bhcd,bhde->bhce', w_c, s)
        o = jnp.einsum('bhcd,bhde->bhce', qd_c, s) + jnp.einsum('bhij,bhje->bhie', qk_c, v_new)
        s = s * cd_c[..., None, None] + jnp.einsum('bhcd,bhce->bhde', kt_c, v_new)
        return s, o

    s_final, o = lax.scan(step, s0, (w, u, qk, q_dec, k_tail, chunk_decay))
    o = jnp.swapaxes(jnp.moveaxis(o, 0, 1), 2, 3).reshape(B, T, H, dv)
    return o, s_final


def gated_delta_mixer(qkv, gate, beta_logit, decay_logit, conv_w, a_log, dt_bias, norm_g, s0):
    B, T = qkv.shape[:2]
    qkv = jax.nn.silu(centred_depthwise_conv(qkv, conv_w))
    q, k, v = split_last(qkv, (DELTA_HEADS * DELTA_DK, DELTA_HEADS * DELTA_DK, DELTA_WIDTH))
    q = l2_normalize(q.reshape(B, T, DELTA_HEADS, DELTA_DK)) * DELTA_DK ** -0.5
    k = l2_normalize(k.reshape(B, T, DELTA_HEADS, DELTA_DK))
    v = v.reshape(B, T, DELTA_HEADS, DELTA_DV)
    beta = jax.nn.sigmoid(beta_logit)
    g = -jnp.exp(a_log) * jax.nn.softplus(decay_logit + dt_bias)
    o_f, s_f = chunk_gated_delta_rule(q, k, v, beta[:, :, 0], g[:, :, 0], s0[:, 0])
    rev = lambda t: jnp.flip(t, 1)
    o_b, s_b = chunk_gated_delta_rule(rev(q), rev(k), rev(v), rev(beta[:, :, 1]), rev(g[:, :, 1]), s0[:, 1])
    o = o_f + rev(o_b)
    o = rms_norm(o, norm_g) * jax.nn.silu(gate.reshape(B, T, DELTA_HEADS, DELTA_DV))
    return o.reshape(B, T, DELTA_WIDTH), jnp.stack([s_f, s_b], axis=1)


def gqa_block(qb, k, v):
    s = jnp.einsum('bqngd,bknd->bngqk', qb, k) * HEAD_DIM ** -0.5
    p = jax.nn.softmax(s, axis=-1)
    return jnp.einsum('bngqk,bknd->bqngd', p, v)


def expert_choice_ffn(h, router_w, w1, w3, w2):
    B, T, D = h.shape
    n = B * T
    cap = max(1, EC_CAPACITY_FACTOR * n // N_EXPERTS)
    xf = h.reshape(n, D)
    aff = jax.nn.softmax(jnp.dot(xf, router_w).astype(jnp.float32), axis=-1)
    gate, idx = lax.top_k(aff.T, cap)
    xe = xf[idx]
    hid = jax.nn.silu(jnp.einsum('ecd,edf->ecf', xe, w1)) * jnp.einsum('ecd,edf->ecf', xe, w3)
    ye = jnp.einsum('ecf,efd->ecd', hid, w2) * gate[..., None].astype(h.dtype)
    out = jnp.zeros_like(xf).at[idx.reshape(-1)].add(ye.reshape(-1, D).astype(xf.dtype))
    return out.reshape(B, T, D)


def even_mixer(h, l, P, rope, ctx):
    li = l // 2
    B, T = h.shape[:2]
    f32 = jnp.float32
    z = jnp.dot(h, P['w_in_ab'][li]).astype(f32)
    xa, ga, qd, kd, vd = split_last(z, (LRU_WIDTH, LRU_WIDTH, DIFF_QK, DIFF_QK, DIFF_V))
    h0 = jnp.zeros((B, 2, LRU_WIDTH), f32) if ctx is None else ctx[0].astype(f32)
    y_a, lru_state = rglru_mixer(
        xa, ga, P['lru_conv_w'][li], P['lru_conv_b'][li], P['lru_wa'][li], P['lru_ba'][li],
        P['lru_wi'][li], P['lru_bi'][li], P['lru_lam'][li], h0)
    q = rms_norm(qd.reshape(B, T, DIFF_HEADS, 2, HEAD_DIM), P['diff_q_gain'][li])
    k = rms_norm(kd.reshape(B, T, DIFF_HEADS, 2, HEAD_DIM), P['diff_k_gain'][li])
    v = vd.reshape(B, T, DIFF_HEADS, 2 * HEAD_DIM)
    new = (lru_state, k, v) if ctx is None else None
    if ctx is not None:
        q = apply_rope(q, rope[0], rope[1])
        k = jnp.concatenate([ctx[1].astype(f32), apply_rope(k, rope[0], rope[1])], axis=1)
        v = jnp.concatenate([ctx[2].astype(f32), v], axis=1)
    lam_init = 0.8 - 0.6 * math.exp(-0.3 * l)
    lv = P['diff_lam'][li].astype(f32)
    lam = jnp.exp(jnp.sum(lv[0] * lv[1])) - jnp.exp(jnp.sum(lv[2] * lv[3])) + lam_init
    o = sweep_query_blocks(lambda qb: diff_attention_block(qb, k, v, lam), q)
    y_b = rms_norm(o, P['diff_sub_gain'][li]) * (1.0 - lam_init)
    y = jnp.concatenate([y_a, y_b.reshape(B, T, DIFF_V)], axis=-1).astype(h.dtype)
    return jnp.dot(y, P['w_out'][l]), new


def odd_mixer(h, l, P, rope, ctx):
    li = l // 2
    B, T = h.shape[:2]
    f32 = jnp.float32
    z = jnp.dot(h, P['w_in_cd'][li]).astype(f32)
    qkv, gate, beta_logit, decay_logit, qg, kg, vg = split_last(
        z, (DELTA_CONV_CH, DELTA_WIDTH, 2 * DELTA_HEADS, 2 * DELTA_HEADS, GQA_WIDTH,
            GQA_KV_HEADS * HEAD_DIM, GQA_KV_HEADS * HEAD_DIM))
    s0 = jnp.zeros((B, 2, DELTA_HEADS, DELTA_DK, DELTA_DV), f32) if ctx is None else ctx[0].astype(f32)
    y_c, delta_state = gated_delta_mixer(
        qkv, gate, beta_logit.reshape(B, T, 2, DELTA_HEADS), decay_logit.reshape(B, T, 2, DELTA_HEADS),
        P['delta_conv_w'][li], P['delta_a_log'][li], P['delta_dt_bias'][li], P['delta_norm_g'][li], s0)
    q = rms_norm(qg.reshape(B, T, GQA_KV_HEADS, GQA_GROUP, HEAD_DIM), P['gqa_q_gain'][li])
    k = rms_norm(kg.reshape(B, T, GQA_KV_HEADS, HEAD_DIM), P['gqa_k_gain'][li])
    v = vg.reshape(B, T, GQA_KV_HEADS, HEAD_DIM)
    new = (delta_state, k, v) if ctx is None else None
    if ctx is not None:
        q = apply_rope(q, rope[0], rope[1])
        k = jnp.concatenate([ctx[1].astype(f32), apply_rope(k, rope[0], rope[1])], axis=1)
        v = jnp.concatenate([ctx[2].astype(f32), v], axis=1)
    o = sweep_query_blocks(lambda qb: gqa_block(qb, k, v), q)
    y = jnp.concatenate([y_c, o.reshape(B, T, GQA_WIDTH)], axis=-1).astype(h.dtype)
    return jnp.dot(y, P['w_out'][l]), new


def run_layer(x, mod, l, P, rope, ctx):
    shift1, scale1, gate1, shift2, scale2, gate2 = jnp.split(mod, 6, axis=-1)
    h = rms_norm(x, P['norm1_g'][l]) * (1 + scale1) + shift1
    if l % 2 == 0:
        y, new = even_mixer(h, l, P, rope, ctx)
    else:
        y, new = odd_mixer(h, l, P, rope, ctx)
    x = x + gate1 * y
    h = rms_norm(x, P['norm2_g'][l]) * (1 + scale2) + shift2
    x = x + gate2 * expert_choice_ffn(h, P['router_w'][l], P['exp_w1'][l], P['exp_w3'][l], P['exp_w2'][l])
    return x, new


def setup_inputs(seed: int = 0) -> dict:
    key = jax.random.key(seed)
    keys = jax.random.split(key, 48)
    counter = [0]
    f32 = jnp.float32

    def nk():
        counter[0] += 1
        return keys[counter[0] - 1]

    def nrm(shape, scale):
        return scale * jax.random.normal(nk(), shape, f32)

    def gain(shape):
        return 1.0 + nrm(shape, 0.01)

    a0 = jax.random.uniform(nk(), (N_AB_LAYERS, 2, LRU_WIDTH), f32, 0.9, 0.999)
    a_root = a0 ** (1.0 / LRU_C)
    lru_lam = jnp.log(a_root) - jnp.log1p(-a_root)
    a_log = jnp.log(jax.random.uniform(nk(), (N_CD_LAYERS, 2, DELTA_HEADS), f32, 1.0, 16.0))
    dt = jnp.exp(jax.random.uniform(nk(), (N_CD_LAYERS, 2, DELTA_HEADS), f32, math.log(0.001), math.log(0.1)))
    dt_bias = dt + jnp.log(-jnp.expm1(-dt))
    return {
        'x_prompt': nrm((BATCH, SEQ, D_MODEL), 1.0),
        'x_sample': nrm((DEC_BATCH, DEC_SEQ, D_MODEL), 1.0),
        'state_lru': nrm((DEC_BATCH, N_AB_LAYERS, 2, LRU_WIDTH), 0.5),
        'cache_diff_k': nrm((DEC_BATCH, N_AB_LAYERS, PAST_LEN, DIFF_HEADS, 2, HEAD_DIM), 1.0),
        'cache_diff_v': nrm((DEC_BATCH, N_AB_LAYERS, PAST_LEN, DIFF_HEADS, 2 * HEAD_DIM), 1.0),
        'state_delta': nrm((DEC_BATCH, N_CD_LAYERS, 2, DELTA_HEADS, DELTA_DK, DELTA_DV), DELTA_DK ** -0.5),
        'cache_gqa_k': nrm((DEC_BATCH, N_CD_LAYERS, PAST_LEN, GQA_KV_HEADS, HEAD_DIM), 1.0),
        'cache_gqa_v': nrm((DEC_BATCH, N_CD_LAYERS, PAST_LEN, GQA_KV_HEADS, HEAD_DIM), 1.0),
        'c': nrm((DEC_BATCH, D_MODEL), 1.0),
        'c_ctx': nrm((D_MODEL,), 1.0),
        'norm1_g': gain((DEPTH, D_MODEL)),
        'norm2_g': gain((DEPTH, D_MODEL)),
        'w_mod': nrm((DEPTH, D_MODEL, 6 * D_MODEL), 0.5 * D_MODEL ** -0.5),
        'b_mod': nrm((DEPTH, 6 * D_MODEL), 0.01),
        'w_in_ab': nrm((N_AB_LAYERS, D_MODEL, IN_AB), D_MODEL ** -0.5),
        'lru_conv_w': nrm((N_AB_LAYERS, CONV_W, LRU_WIDTH), 0.5),
        'lru_conv_b': nrm((N_AB_LAYERS, LRU_WIDTH), 0.01),
        'lru_wa': nrm((N_AB_LAYERS, 2, LRU_BLOCKS, LRU_BW, LRU_BW), LRU_BW ** -0.5),
        'lru_ba': nrm((N_AB_LAYERS, 2, LRU_WIDTH), 0.01),
        'lru_wi': nrm((N_AB_LAYERS, 2, LRU_BLOCKS, LRU_BW, LRU_BW), LRU_BW ** -0.5),
        'lru_bi': nrm((N_AB_LAYERS, 2, LRU_WIDTH), 0.01),
        'lru_lam': lru_lam,
        'diff_q_gain': gain((N_AB_LAYERS, HEAD_DIM)),
        'diff_k_gain': gain((N_AB_LAYERS, HEAD_DIM)),
        'diff_lam': nrm((N_AB_LAYERS, 4, HEAD_DIM), 0.1),
        'diff_sub_gain': gain((N_AB_LAYERS, 2 * HEAD_DIM)),
        'w_in_cd': nrm((N_CD_LAYERS, D_MODEL, IN_CD), D_MODEL ** -0.5),
        'delta_conv_w': nrm((N_CD_LAYERS, CONV_W, DELTA_CONV_CH), 0.5),
        'delta_a_log': a_log,
        'delta_dt_bias': dt_bias,
        'delta_norm_g': gain((N_CD_LAYERS, DELTA_DV)),
        'gqa_q_gain': gain((N_CD_LAYERS, HEAD_DIM)),
        'gqa_k_gain': gain((N_CD_LAYERS, HEAD_DIM)),
        'w_out': nrm((DEPTH, MIX_WIDTH, D_MODEL), MIX_WIDTH ** -0.5),
        'router_w': nrm((DEPTH, D_MODEL, N_EXPERTS), D_MODEL ** -0.5),
        'exp_w1': nrm((DEPTH, N_EXPERTS, D_MODEL, EXPERT_FF), D_MODEL ** -0.5),
        'exp_w3': nrm((DEPTH, N_EXPERTS, D_MODEL, EXPERT_FF), D_MODEL ** -0.5),
        'exp_w2': nrm((DEPTH, N_EXPERTS, EXPERT_FF, D_MODEL), EXPERT_FF ** -0.5),
    }


def reference(x_prompt, x_sample, state_lru, cache_diff_k, cache_diff_v, state_delta, cache_gqa_k, cache_gqa_v,
              c, c_ctx, norm1_g, norm2_g, w_mod, b_mod, w_in_ab, lru_conv_w, lru_conv_b, lru_wa, lru_ba,
              lru_wi, lru_bi, lru_lam, diff_q_gain, diff_k_gain, diff_lam, diff_sub_gain, w_in_cd,
              delta_conv_w, delta_a_log, delta_dt_bias, delta_norm_g, gqa_q_gain, gqa_k_gain, w_out,
              router_w, exp_w1, exp_w3, exp_w2):
    P = dict(
        norm1_g=norm1_g, norm2_g=norm2_g, w_in_ab=w_in_ab, lru_conv_w=lru_conv_w, lru_conv_b=lru_conv_b,
        lru_wa=lru_wa, lru_ba=lru_ba, lru_wi=lru_wi, lru_bi=lru_bi, lru_lam=lru_lam,
        diff_q_gain=diff_q_gain, diff_k_gain=diff_k_gain, diff_lam=diff_lam, diff_sub_gain=diff_sub_gain,
        w_in_cd=w_in_cd, delta_conv_w=delta_conv_w, delta_a_log=delta_a_log, delta_dt_bias=delta_dt_bias,
        delta_norm_g=delta_norm_g, gqa_q_gain=gqa_q_gain, gqa_k_gain=gqa_k_gain, w_out=w_out,
        router_w=router_w, exp_w1=exp_w1, exp_w3=exp_w3, exp_w2=exp_w2)
    dtype = x_prompt.dtype

    y_prompt = x_prompt
    lru_new, dk_new, dv_new, delta_new, gk_new, gv_new = [], [], [], [], [], []
    for l in range(DEPTH):
        mod = (jnp.dot(jax.nn.silu(c_ctx), w_mod[l]) + b_mod[l])[None, None, :]
        y_prompt, new = run_layer(y_prompt, mod, l, P, None, None)
        if l % 2 == 0:
            lru_new.append(new[0])
            dk_new.append(new[1])
            dv_new.append(new[2])
        else:
            delta_new.append(new[0])
            gk_new.append(new[1])
            gv_new.append(new[2])

    rope = axial_rope_tables(x_sample.shape[1], HEAD_DIM)
    y_sample = x_sample
    for l in range(DEPTH):
        li = l // 2
        mod = (jnp.dot(jax.nn.silu(c), w_mod[l]) + b_mod[l])[:, None, :]
        if l % 2 == 0:
            ctx = (state_lru[:, li], cache_diff_k[:, li], cache_diff_v[:, li])
        else:
            ctx = (state_delta[:, li], cache_gqa_k[:, li], cache_gqa_v[:, li])
        y_sample, _ = run_layer(y_sample, mod, l, P, rope, ctx)

    new_state_lru = jnp.stack(lru_new, axis=1).astype(dtype)
    new_cache_diff_k = jnp.stack(dk_new, axis=1).astype(dtype)
    new_cache_diff_v = jnp.stack(dv_new, axis=1).astype(dtype)
    new_state_delta = jnp.stack(delta_new, axis=1).astype(dtype)
    new_cache_gqa_k = jnp.stack(gk_new, axis=1).astype(dtype)
    new_cache_gqa_v = jnp.stack(gv_new, axis=1).astype(dtype)
    return (y_prompt, y_sample, new_state_lru, new_cache_diff_k, new_cache_diff_v, new_state_delta, new_cache_gqa_k, new_cache_gqa_v)
```

```python
import functools
import math

import jax
import jax.numpy as jnp
from jax import lax
from jax.experimental import pallas as pl
from jax.experimental.pallas import tpu as pltpu

F32 = jnp.float32
BF16 = jnp.bfloat16

D_MODEL = 1024
DEPTH = 4
GRID_W = 64
HEAD_DIM = 64
ROPE_THETA = 10000.0
RMS_EPS = 1e-6
LRU_WIDTH = 512
LRU_BLOCKS = 8
LRU_C = 8.0
DIFF_HEADS = 4
DELTA_HEADS = 4
DELTA_DK = 128
GQA_HEADS = 8
GQA_KV_HEADS = 2
N_EXPERTS = 16
EC_CAPACITY_FACTOR = 2
EXPERT_FF = 512
IN_AB = 2560
IN_CD_PAD = 2944

LANES = 128
SUBLANES = 8
VMEM_LIMIT = 56 * 1024 * 1024

Q_SCALE = HEAD_DIM ** -0.5 * math.log2(math.e)
CHUNK = 128
TOK_TILE = 128
DISPATCH_WIN = TOK_TILE + SUBLANES
COMBINE_WIN = 256


def _cparams(*sem):
    return pltpu.CompilerParams(dimension_semantics=sem, vmem_limit_bytes=VMEM_LIMIT)


def _mm(a, b, dims=(((1,), (0,)), ((), ()))):
    return lax.dot_general(a, b, dims, preferred_element_type=F32)


_NT = (((1,), (1,)), ((), ()))
_TN = (((0,), (0,)), ((), ()))


def _dot1(a, b, dims=(((1,), (0,)), ((), ()))):
    return _mm(a.astype(BF16), b.astype(BF16), dims)


def _split(x):
    hi = x.astype(BF16)
    lo = (x - hi.astype(F32)).astype(BF16)
    return hi, lo


def _dot3(a, b, dims=(((1,), (0,)), ((), ()))):
    ah, al = _split(a)
    bh, bl = _split(b)
    return _mm(ah, bh, dims) + (_mm(ah, bl, dims) + _mm(al, bh, dims))


def _dot_exact_rhs(m_bf16, x):
    x1 = x.astype(BF16)
    r1 = x - x1.astype(F32)
    x2 = r1.astype(BF16)
    x3 = (r1 - x2.astype(F32)).astype(BF16)
    return _mm(m_bf16, x1) + (_mm(m_bf16, x2) + _mm(m_bf16, x3))


def _dot_exact_lhs(x, m_bf16):
    x1 = x.astype(BF16)
    r1 = x - x1.astype(F32)
    x2 = r1.astype(BF16)
    x3 = (r1 - x2.astype(F32)).astype(BF16)
    return _mm(x1, m_bf16) + (_mm(x2, m_bf16) + _mm(x3, m_bf16))


def _rms(x, g):
    return x * lax.rsqrt(jnp.mean(x * x, axis=-1, keepdims=True) + RMS_EPS) * g


def _mod_body(c_ref, w_ref, b_ref, o_ref):
    c = c_ref[...]
    a = c * jax.nn.sigmoid(c)
    o_ref[0] = _dot3(a, w_ref[0]) + b_ref[0]


def _mod_call(cs, w_mod, b_mod):
    rows, d = cs.shape
    width = w_mod.shape[2]
    tn = 512
    return pl.pallas_call(
        _mod_body,
        grid=(DEPTH, width // tn),
        in_specs=[
            pl.BlockSpec((rows, d), lambda l, j: (0, 0)),
            pl.BlockSpec((1, d, tn), lambda l, j: (l, 0, j)),
            pl.BlockSpec((1, 1, tn), lambda l, j: (l, 0, j)),
        ],
        out_specs=pl.BlockSpec((1, rows, tn), lambda l, j: (l, 0, j)),
        out_shape=jax.ShapeDtypeStruct((DEPTH, rows, width), F32),
        compiler_params=_cparams("parallel", "parallel"),
        name="mod",
    )(cs, w_mod, b_mod.reshape(DEPTH, 1, width))


def _inproj_body(x_ref, g_ref, sc_ref, sh_ref, w_ref, z_ref):
    h = _rms(x_ref[...], g_ref[...]) * (1.0 + sc_ref[0]) + sh_ref[0]
    z_ref[...] = _mm(h.astype(BF16), w_ref[...])


def _inproj(x, g, scale, shift, w, rows_per_mod):
    n, d = x.shape
    width = w.shape[1]
    tm = 256
    per = rows_per_mod // tm
    return pl.pallas_call(
        _inproj_body,
        grid=(n // tm,),
        in_specs=[
            pl.BlockSpec((tm, d), lambda i: (i, 0)),
            pl.BlockSpec((1, d), lambda i: (0, 0)),
            pl.BlockSpec((1, 1, d), lambda i: (i // per, 0, 0)),
            pl.BlockSpec((1, 1, d), lambda i: (i // per, 0, 0)),
            pl.BlockSpec((d, width), lambda i: (0, 0)),
        ],
        out_specs=pl.BlockSpec((tm, width), lambda i: (i, 0)),
        out_shape=jax.ShapeDtypeStruct((n, width), F32),
        compiler_params=_cparams("parallel"),
        name="inproj",
    )(x, g, scale, shift, w)


def _shift_rows(x, k, t_iota, seq):
    rows = x.shape[0]
    if k > 0:
        return jnp.where(t_iota >= k, pltpu.roll(x, k, 0), 0.0)
    return jnp.where(t_iota < seq + k, pltpu.roll(x, rows + k, 0), 0.0)


def _conv4(x, w_ref, t_iota, seq):
    return (
        w_ref[0:1, :] * _shift_rows(x, 2, t_iota, seq)
        + w_ref[1:2, :] * _shift_rows(x, 1, t_iota, seq)
        + w_ref[2:3, :] * x
        + w_ref[3:4, :] * _shift_rows(x, -1, t_iota, seq)
    )


def _tile_scan(a, b, row, reverse):
    for d in (1, 2, 4):
        if reverse:
            ok = row < SUBLANES - d
            a_sh = pltpu.roll(a, SUBLANES - d, 0)
            b_sh = pltpu.roll(b, SUBLANES - d, 0)
        else:
            ok = row >= d
            a_sh = pltpu.roll(a, d, 0)
            b_sh = pltpu.roll(b, d, 0)
        a_sh = jnp.where(ok, a_sh, 1.0)
        b_sh = jnp.where(ok, b_sh, 0.0)
        b = a * b_sh + b
        a = a * a_sh
    return a, b


def _lru_body(xa_ref, ga_ref, cw_ref, cb_ref, wa_ref, ba_ref, wi_ref, bi_ref, lam_ref, h0_ref,
              y_ref, st_ref, af, bf, ab, bb, hf, hb):
    rows = xa_ref.shape[0]
    t_iota = lax.broadcasted_iota(jnp.int32, (rows, LANES), 0)
    u = _conv4(xa_ref[...], cw_ref, t_iota, rows) + cb_ref[...]
    ub = u.astype(BF16)
    for d, (a_s, b_s) in enumerate(((af, bf), (ab, bb))):
        r = jax.nn.sigmoid(_mm(ub, wa_ref[d].astype(BF16)) + ba_ref[d:d + 1, :])
        i = jax.nn.sigmoid(_mm(ub, wi_ref[d].astype(BF16)) + bi_ref[d:d + 1, :])
        log_a = -LRU_C * r * jax.nn.softplus(-lam_ref[d:d + 1, :])
        a = jnp.exp(log_a)
        a_s[...] = a
        b_s[...] = jnp.sqrt(1.0 - a * a) * (i * u)

    n8 = rows // SUBLANES
    row = lax.broadcasted_iota(jnp.int32, (SUBLANES, LANES), 0)

    def step(i, carry):
        h_f, h_b = carry
        r0 = pl.multiple_of(i * SUBLANES, SUBLANES)
        a_c, b_c = _tile_scan(af[pl.ds(r0, SUBLANES), :], bf[pl.ds(r0, SUBLANES), :], row, False)
        hf_t = a_c * h_f + b_c
        hf[pl.ds(r0, SUBLANES), :] = hf_t
        r1 = pl.multiple_of((n8 - 1 - i) * SUBLANES, SUBLANES)
        a_c, b_c = _tile_scan(ab[pl.ds(r1, SUBLANES), :], bb[pl.ds(r1, SUBLANES), :], row, True)
        hb_t = a_c * h_b + b_c
        hb[pl.ds(r1, SUBLANES), :] = hb_t
        return hf_t[SUBLANES - 1:SUBLANES, :], hb_t[0:1, :]

    h_f, h_b = lax.fori_loop(0, n8, step, (h0_ref[0, 0:1, :], h0_ref[0, 1:2, :]), unroll=2)
    y_ref[...] = ((hf[...] + hb[...]) * jax.nn.gelu(ga_ref[...])).astype(BF16)
    st_ref[0, 0:1, :] = h_f
    st_ref[0, 1:2, :] = h_b


def _lru(z, batch, conv_w, conv_b, wa_bd, ba, wi_bd, bi, lam, h0):
    n = z.shape[0]
    t = n // batch
    nj = LRU_WIDTH // LANES
    return pl.pallas_call(
        _lru_body,
        grid=(batch, nj),
        in_specs=[
            pl.BlockSpec((t, LANES), lambda b, j: (b, j)),
            pl.BlockSpec((t, LANES), lambda b, j: (b, nj + j)),
            pl.BlockSpec((4, LANES), lambda b, j: (0, j)),
            pl.BlockSpec((1, LANES), lambda b, j: (0, j)),
            pl.BlockSpec((2, None, LANES, LANES), lambda b, j: (0, j, 0, 0)),
            pl.BlockSpec((2, LANES), lambda b, j: (0, j)),
            pl.BlockSpec((2, None, LANES, LANES), lambda b, j: (0, j, 0, 0)),
            pl.BlockSpec((2, LANES), lambda b, j: (0, j)),
            pl.BlockSpec((2, LANES), lambda b, j: (0, j)),
            pl.BlockSpec((1, 2, LANES), lambda b, j: (b, 0, j)),
        ],
        out_specs=[
            pl.BlockSpec((t, LANES), lambda b, j: (b, j)),
            pl.BlockSpec((1, 2, LANES), lambda b, j: (b, 0, j)),
        ],
        out_shape=[
            jax.ShapeDtypeStruct((n, LRU_WIDTH), BF16),
            jax.ShapeDtypeStruct((batch, 2, LRU_WIDTH), F32),
        ],
        scratch_shapes=[pltpu.VMEM((t, LANES), F32)] * 6,
        compiler_params=_cparams("parallel", "parallel"),
        name="lru",
    )(z, z, conv_w, conv_b, wa_bd, ba, wi_bd, bi, lam, h0)


def _head_norm(x, g_ref, gm_ref):
    outs = []
    for j in range(x.shape[1] // LANES):
        xs = x[:, j * LANES:(j + 1) * LANES]
        sq = xs * xs
        hi, lo = _split(sq)
        ms = _mm(hi, gm_ref[...]) + _mm(lo, gm_ref[...])
        outs.append(xs * lax.rsqrt(ms + RMS_EPS) * g_ref[:, j * LANES:(j + 1) * LANES])
    return outs


def _rope(xs, cos, sin, first_half):
    swapped = jnp.where(first_half, pltpu.roll(xs, LANES - HEAD_DIM // 2, 1), pltpu.roll(xs, HEAD_DIM // 2, 1))
    return xs * cos + swapped * sin


def _prep_ctx_body(q_ref, k_ref, gq_ref, gk_ref, gm_ref, qo_ref, kc_ref, kb_ref):
    qs = _head_norm(q_ref[...], gq_ref, gm_ref)
    ks = _head_norm(k_ref[...], gk_ref, gm_ref)
    for j, xs in enumerate(qs):
        qo_ref[:, j * LANES:(j + 1) * LANES] = (xs * Q_SCALE).astype(BF16)
    for j, xs in enumerate(ks):
        kc_ref[:, j * LANES:(j + 1) * LANES] = xs
        kb_ref[:, j * LANES:(j + 1) * LANES] = xs.astype(BF16)


def _prep_rope_body(q_ref, k_ref, gq_ref, gk_ref, gm_ref, cos_ref, sin_ref, qo_ref, kb_ref):
    qs = _head_norm(q_ref[...], gq_ref, gm_ref)
    ks = _head_norm(k_ref[...], gk_ref, gm_ref)
    cos = cos_ref[...]
    sin = sin_ref[...]
    lane = lax.broadcasted_iota(jnp.int32, cos.shape, 1)
    first_half = (lane % HEAD_DIM) < HEAD_DIM // 2
    for j, xs in enumerate(qs):
        qo_ref[:, j * LANES:(j + 1) * LANES] = (_rope(xs, cos, sin, first_half) * Q_SCALE).astype(BF16)
    for j, xs in enumerate(ks):
        kb_ref[:, j * LANES:(j + 1) * LANES] = _rope(xs, cos, sin, first_half).astype(BF16)


def _prep(z, q_blk, k_blk, kw, gq, gk, gm, rope, seq):
    n = z.shape[0]
    tm = 256
    qw = 512
    in_specs = [
        pl.BlockSpec((tm, qw), lambda i: (i, q_blk)),
        pl.BlockSpec((tm, kw), lambda i: (i, k_blk)),
        pl.BlockSpec((1, qw), lambda i: (0, 0)),
        pl.BlockSpec((1, kw), lambda i: (0, 0)),
        pl.BlockSpec((LANES, LANES), lambda i: (0, 0)),
    ]
    q_out = (pl.BlockSpec((tm, qw), lambda i: (i, 0)), jax.ShapeDtypeStruct((n, qw), BF16))
    kb_out = (pl.BlockSpec((tm, kw), lambda i: (i, 0)), jax.ShapeDtypeStruct((n, kw), BF16))
    if rope is None:
        kc_out = (pl.BlockSpec((tm, kw), lambda i: (i, 0)), jax.ShapeDtypeStruct((n, kw), F32))
        outs = (q_out, kc_out, kb_out)
        body = _prep_ctx_body
        args = (z, z, gq, gk, gm)
    else:
        per = seq // tm
        in_specs += [pl.BlockSpec((tm, LANES), lambda i: (i % per, 0))] * 2
        outs = (q_out, kb_out)
        body = _prep_rope_body
        args = (z, z, gq, gk, gm, rope[0], rope[1])
    return pl.pallas_call(
        body,
        grid=(n // tm,),
        in_specs=in_specs,
        out_specs=[o[0] for o in outs],
        out_shape=[o[1] for o in outs],
        compiler_params=_cparams("parallel"),
        name="prep",
    )(*args)


def _attn_body(q_ref, k_ref, v_ref, lam_ref, sg_ref, o_ref, *, diff, lam_init, tk):
    tq = q_ref.shape[0]
    q = q_ref[...]
    lane = lax.broadcasted_iota(jnp.int32, (tq, LANES), 1)
    zero = jnp.zeros_like(q)
    q2 = jnp.concatenate([jnp.where(lane < HEAD_DIM, q, zero), jnp.where(lane >= HEAD_DIM, q, zero)], axis=0)
    nkb = k_ref.shape[1] // tk

    def scores(kb):
        r0 = pl.multiple_of(kb * tk, tk)
        return _mm(q2, k_ref[0, pl.ds(r0, tk), :], _NT), r0

    def max_blk(kb, rm):
        s, _ = scores(kb)
        for c in range(tk // LANES):
            rm = jnp.maximum(rm, s[:, c * LANES:(c + 1) * LANES])
        return rm

    rm = lax.fori_loop(0, nkb, max_blk, jnp.full((2 * tq, LANES), -jnp.inf, F32), unroll=True)
    m = jnp.max(rm, axis=-1, keepdims=True)

    ones = jnp.ones((tk, LANES), BF16)

    def acc_blk(kb, carry):
        l, acc = carry
        s, r0 = scores(kb)
        p = jnp.exp2(s - m).astype(BF16)
        return l + _mm(p, ones), acc + _mm(p, v_ref[0, pl.ds(r0, tk), :])

    zeros = jnp.zeros((2 * tq, LANES), F32)
    l, acc = lax.fori_loop(0, nkb, acc_blk, (zeros, zeros), unroll=True)
    o2 = acc / l
    o0 = o2[:tq]
    o1 = o2[tq:]
    if diff:
        lv = lam_ref[...]
        lam = (jnp.exp(jnp.sum(lv[0:1] * lv[1:2], axis=-1, keepdims=True))
               - jnp.exp(jnp.sum(lv[2:3] * lv[3:4], axis=-1, keepdims=True)) + lam_init)
        y = _rms(o0 - lam * o1, sg_ref[...]) * (1.0 - lam_init)
    else:
        y = jnp.where(lane < HEAD_DIM, o0, o1)
    o_ref[...] = y.astype(BF16)


def _attn(qn, kk, vv, lam_p, sub_gain, batch, diff, lam_init):
    n = qn.shape[0]
    t = n // batch
    tq = 256
    nq = t // tq
    tkv = kk.shape[1]
    tk = min(512, tkv)
    kv_blocks = kk.shape[2] // LANES
    share = 4 // kv_blocks
    return pl.pallas_call(
        functools.partial(_attn_body, diff=diff, lam_init=lam_init, tk=tk),
        grid=(batch, 4, nq),
        in_specs=[
            pl.BlockSpec((tq, LANES), lambda b, j, i: (b * nq + i, j)),
            pl.BlockSpec((1, tkv, LANES), lambda b, j, i: (b, 0, j // share)),
            pl.BlockSpec((1, tkv, LANES), lambda b, j, i: (b, 0, j // share)),
            pl.BlockSpec((4, HEAD_DIM), lambda b, j, i: (0, 0)),
            pl.BlockSpec((1, LANES), lambda b, j, i: (0, 0)),
        ],
        out_specs=pl.BlockSpec((tq, LANES), lambda b, j, i: (b * nq + i, j)),
        out_shape=jax.ShapeDtypeStruct((n, 4 * LANES), BF16),
        compiler_params=_cparams("parallel", "parallel", "parallel"),
        name="attn",
    )(qn, kk, vv, lam_p, sub_gain)


INV_BASE = 8


_BNN = (((2,), (1,)), ((0,), (0,)))
_BNT = (((2,), (2,)), ((0,), (0,)))


def _bdot3(a, b):
    ah, al = _split(a)
    bh, bl = _split(b)
    return _mm(ah, bh, _BNN) + (_mm(ah, bl, _BNN) + _mm(al, bh, _BNN))


def _unit_tri_inverse(a, eye, blocks):
    same_base, level_masks = blocks
    d = jnp.where(same_base, a, 0.0)
    p = eye - d
    q = _bdot3(d, d)
    p = p + _bdot3(p, q)
    q = _bdot3(q, q)
    x = p + _bdot3(p, q)
    for m in level_masks:
        x = x - _bdot3(_bdot3(x, jnp.where(m, a, 0.0)), x)
    return x


def _delta_prepare(qc, kc, vc, cc, n_fwd, consts):
    incl, strict, cum_m, eye, blocks = consts
    x1 = cc.astype(BF16)
    r1 = cc - x1.astype(F32)
    x2 = r1.astype(BF16)
    x3 = (r1 - x2.astype(F32)).astype(BF16)
    cum = _mm(cum_m, x1, _BNN) + (_mm(cum_m, x2, _BNN) + _mm(cum_m, x3, _BNN))
    cum_t = jnp.swapaxes(cum, 1, 2)
    g_col = cum[:, :, 2:3]
    g_row = cum_t[:, 2:3, :]
    beta = cc[:, :, 0:1]
    g_last = jnp.concatenate([g_row[:n_fwd, :, CHUNK - 1:CHUNK], g_row[n_fwd:, :, 0:1]], axis=0)
    decay = jnp.exp(jnp.where(incl > 0.0, g_col - g_row, -jnp.inf))
    kb = kc * beta
    kcb = kc.astype(BF16)
    a = jnp.where(strict > 0.0, _mm(kb.astype(BF16), kcb, _BNT) * decay, 0.0)
    inv = _unit_tri_inverse(a, eye, blocks)
    e_g = jnp.exp(g_col)
    sol = _bdot3(inv, jnp.concatenate([kb * e_g, vc * beta], axis=2))
    w = sol[:, :, :DELTA_DK]
    u = sol[:, :, DELTA_DK:]
    qk = jnp.where(incl > 0.0, _mm(qc.astype(BF16), kcb, _BNT) * decay, 0.0)
    k_tail = kc * jnp.exp(g_last - g_col)
    return (w.astype(BF16), u, qk.astype(BF16), (qc * e_g).astype(BF16), k_tail.astype(BF16), jnp.exp(g_last))


def _delta_apply(s, pre, g):
    w, u, qk, q_dec, k_tail, chunk_decay = (t[g] for t in pre)
    sb = s.astype(BF16)
    v_new = u - _mm(w, sb)
    vb = v_new.astype(BF16)
    o = _mm(q_dec, sb) + _mm(qk, vb)
    return o, s * chunk_decay + _mm(k_tail, vb, _TN)


def _delta_body(q_ref, k_ref, v_ref, gate_ref, lg_ref, cwq_ref, cwk_ref, cwv_ref, alog_ref, dtb_ref, ng_ref,
                s0_ref, o_ref, st_ref, qs, ks, vs, cs_f, cs_b, o_f, o_b, *, seq, group):
    rows = q_ref.shape[0]
    head = pl.program_id(1)
    t_iota = lax.broadcasted_iota(jnp.int32, (rows, LANES), 0) % seq
    lane = lax.broadcasted_iota(jnp.int32, (rows, LANES), 1)

    def conv_silu(ref, cw):
        x = _conv4(ref[...], cw, t_iota, seq)
        return x * jax.nn.sigmoid(x)

    q = conv_silu(q_ref, cwq_ref)
    k = conv_silu(k_ref, cwk_ref)
    qs[...] = q * lax.rsqrt(jnp.sum(q * q, axis=-1, keepdims=True) + 1e-6) * DELTA_DK ** -0.5
    ks[...] = k * lax.rsqrt(jnp.sum(k * k, axis=-1, keepdims=True) + 1e-6)
    vs[...] = conv_silu(v_ref, cwv_ref)
    lg = lg_ref[...]
    beta = jax.nn.sigmoid(lg)
    g_all = -jnp.exp(alog_ref[...]) * jax.nn.softplus(lg + dtb_ref[...])

    def col(arr, idx):
        return jnp.sum(jnp.where(lane == idx, arr, 0.0), axis=1, keepdims=True)

    cs_f[...] = jnp.where(lane == 0, col(beta, head), jnp.where(lane == 2, col(g_all, 2 * DELTA_HEADS + head), 0.0))
    cs_b[...] = jnp.where(lane == 0, col(beta, DELTA_HEADS + head),
                          jnp.where(lane == 2, col(g_all, 3 * DELTA_HEADS + head), 0.0))

    n_seq = rows // seq
    half = n_seq * group
    r = lax.broadcasted_iota(jnp.int32, (CHUNK, CHUNK), 0)
    c = lax.broadcasted_iota(jnp.int32, (CHUNK, CHUNK), 1)
    eye = jnp.where(r == c, 1.0, 0.0)
    level_masks = []
    size = INV_BASE
    while size < CHUNK:
        level_masks.append(((r // (2 * size)) == (c // (2 * size))) & ((r // size) != (c // size)))
        size *= 2
    blocks = ((r // INV_BASE) == (c // INV_BASE), level_masks)

    def per_direction(fwd, bwd):
        stack = lambda m: jnp.broadcast_to(jnp.where(m, 1.0, 0.0)[None], (half, CHUNK, CHUNK))
        return jnp.concatenate([stack(fwd), stack(bwd)], axis=0)

    incl = per_direction(r >= c, r <= c)
    consts = (incl, per_direction(r > c, r < c), incl.astype(BF16), eye, blocks)
    n = seq // CHUNK

    def chunk_rows(r0):
        return pl.ds(r0 if isinstance(r0, int) else pl.multiple_of(r0, CHUNK), CHUNK)

    def trip(i, states):
        rows_f = [[b * seq + (i * group + g) * CHUNK for g in range(group)] for b in range(n_seq)]
        rows_b = [[b * seq + (n - 1 - (i * group + g)) * CHUNK for g in range(group)] for b in range(n_seq)]
        chains = [(r0, cs_f) for rb in rows_f for r0 in rb] + [(r0, cs_b) for rb in rows_b for r0 in rb]
        load = lambda ref: jnp.stack([ref[chunk_rows(r0), :] for r0, _ in chains], axis=0)
        cc = jnp.stack([cref[chunk_rows(r0), :] for r0, cref in chains], axis=0)
        pre = _delta_prepare(load(qs), load(ks), load(vs), cc, half, consts)
        out = []
        for b in range(n_seq):
            s_f, s_b = states[2 * b], states[2 * b + 1]
            for g in range(group):
                o, s_f = _delta_apply(s_f, pre, b * group + g)
                o_f[chunk_rows(rows_f[b][g]), :] = o
                o, s_b = _delta_apply(s_b, pre, half + b * group + g)
                o_b[chunk_rows(rows_b[b][g]), :] = o
            out += [s_f, s_b]
        return tuple(out)

    init = tuple(s0_ref[b, d, 0] for b in range(n_seq) for d in range(2))
    final = trip(0, init) if n == group else lax.fori_loop(0, n // group, trip, init)
    for b in range(n_seq):
        st_ref[b, 0, 0] = final[2 * b]
        st_ref[b, 1, 0] = final[2 * b + 1]
    gate = gate_ref[...]
    o_ref[...] = (_rms(o_f[...] + o_b[...], ng_ref[...]) * (gate * jax.nn.sigmoid(gate))).astype(BF16)


DELTA_GROUP = 4
DELTA_BLOCK_ROWS = 1024


def _delta(z, batch, conv_w, alog_l, dtb_l, norm_g, s0):
    n = z.shape[0]
    t = n // batch
    h = DELTA_HEADS
    nb = max(1, DELTA_BLOCK_ROWS // t)
    rows = nb * t
    group = min(DELTA_GROUP, t // CHUNK)
    blk = lambda off: pl.BlockSpec((rows, LANES), lambda b, j: (b, off + j))
    cwb = lambda off: pl.BlockSpec((4, LANES), lambda b, j: (0, off + j))
    lg_blk = 22
    vec = pl.BlockSpec((1, LANES), lambda b, j: (0, 0))
    st_spec = pl.BlockSpec((nb, 2, 1, DELTA_DK, DELTA_DK), lambda b, j: (b, 0, j, 0, 0))
    return pl.pallas_call(
        functools.partial(_delta_body, seq=t, group=group),
        grid=(batch // nb, h),
        in_specs=[blk(0), blk(h), blk(2 * h), blk(3 * h), pl.BlockSpec((rows, LANES), lambda b, j: (b, lg_blk)),
                  cwb(0), cwb(h), cwb(2 * h), vec, vec, vec, st_spec],
        out_specs=[pl.BlockSpec((rows, LANES), lambda b, j: (b, j)), st_spec],
        out_shape=[jax.ShapeDtypeStruct((n, h * LANES), BF16),
                   jax.ShapeDtypeStruct((batch, 2, h, DELTA_DK, DELTA_DK), F32)],
        scratch_shapes=[pltpu.VMEM((rows, LANES), F32)] * 7,
        compiler_params=_cparams("parallel", "parallel"),
        name="delta",
    )(z, z, z, z, z, conv_w, conv_w, conv_w, alog_l, dtb_l, norm_g, s0)


def _outproj_body(x_ref, ya_ref, yb_ref, wa_ref, wb_ref, g1_ref, n2_ref, sc_ref, sh_ref, rw_ref, rwt_ref,
                  xo_ref, h2_ref, aff_ref, afft_ref):
    y = _mm(ya_ref[...], wa_ref[...]) + _mm(yb_ref[...], wb_ref[...])
    x = x_ref[...] + g1_ref[0] * y
    xo_ref[...] = x
    h = _rms(x, n2_ref[...]) * (1.0 + sc_ref[0]) + sh_ref[0]
    h2_ref[...] = h.astype(BF16)
    logits = _dot3(h, rw_ref[...])
    lane = lax.broadcasted_iota(jnp.int32, logits.shape, 1)
    logits = jnp.where(lane < N_EXPERTS, logits, -jnp.inf)
    e = jnp.exp(logits - jnp.max(logits, axis=-1, keepdims=True))
    aff_ref[...] = e / jnp.sum(e, axis=-1, keepdims=True)
    lt = _dot3(rwt_ref[...], h, _NT)
    et = jnp.exp(lt - jnp.max(lt, axis=0, keepdims=True))
    afft_ref[...] = et / jnp.sum(et, axis=0, keepdims=True)


def _outproj(x, ya, yb, wa, wb, gate1, n2, scale2, shift2, rw, rwt, rows_per_mod):
    n, d = x.shape
    tm = 256
    per = rows_per_mod // tm
    half = ya.shape[1]
    row = lambda w: pl.BlockSpec((tm, w), lambda i: (i, 0))
    mod = pl.BlockSpec((1, 1, d), lambda i: (i // per, 0, 0))
    full = lambda a: pl.BlockSpec(a.shape, lambda i: (0,) * a.ndim)
    return pl.pallas_call(
        _outproj_body,
        grid=(n // tm,),
        in_specs=[row(d), row(half), row(half), full(wa), full(wb), mod, full(n2), mod, mod, full(rw), full(rwt)],
        out_specs=[row(d), row(d), row(LANES), pl.BlockSpec((N_EXPERTS, tm), lambda i: (0, i))],
        out_shape=[jax.ShapeDtypeStruct((n, d), F32), jax.ShapeDtypeStruct((n, d), BF16),
                   jax.ShapeDtypeStruct((n, LANES), F32), jax.ShapeDtypeStruct((N_EXPERTS, n), F32)],
        compiler_params=_cparams("parallel"),
        name="outproj",
    )(x, ya, yb, wa, wb, gate1, n2, scale2, shift2, rw, rwt)


def _route_body(aff_ref, pos_ref, off_ref, boff, *, cap):
    nb = aff_ref.shape[1]
    aff = aff_ref[...]

    def count(mask):
        c = jnp.sum(jnp.where(mask, 1.0, 0.0), axis=2, keepdims=True)
        return jnp.sum(c, axis=1, keepdims=True)

    def as_float(bits):
        return lax.bitcast_convert_type(bits, F32)

    top_bit = 29

    def search(i, thr):
        cand = thr | jnp.left_shift(jnp.int32(1), top_bit - i)
        return jnp.where(count(aff >= as_float(cand)) >= cap, cand, thr)

    thr = lax.fori_loop(0, top_bit + 1, search, jnp.zeros((N_EXPERTS, 1, 1), jnp.int32))
    above = aff >= as_float(thr + 1)
    tied = (aff >= as_float(thr)) & jnp.logical_not(above)
    need = cap - count(above)

    r = lax.broadcasted_iota(jnp.int32, (LANES, LANES), 0)
    c = lax.broadcasted_iota(jnp.int32, (LANES, LANES), 1)
    before = jnp.where(r < c, 1.0, 0.0).astype(BF16)
    ones = jnp.ones((LANES, LANES), BF16)
    rb = lax.broadcasted_iota(jnp.int32, (nb, nb), 0)
    cb = lax.broadcasted_iota(jnp.int32, (nb, nb), 1)
    blocks_before = jnp.where(cb < rb, 1.0, 0.0).astype(BF16)

    def excl_cumsum(mask):
        m2 = jnp.where(mask, 1.0, 0.0).astype(BF16).reshape(N_EXPERTS * nb, LANES)
        within = _mm(m2, before).reshape(N_EXPERTS, nb, LANES)
        tot = _mm(m2, ones).astype(BF16).reshape(N_EXPERTS, nb, LANES)
        for e in range(N_EXPERTS):
            boff[e] = _mm(blocks_before, tot[e])
        return within + boff[...]

    sel = above | (tied & (excl_cumsum(tied) < need))
    pos = excl_cumsum(sel)
    pos_ref[...] = jnp.where(sel, pos, -1.0e6).astype(jnp.int32)
    off_ref[...] = boff[...]


def _route(afft, cap):
    e, n = afft.shape
    nb = n // LANES
    shape = (e, nb, LANES)
    return pl.pallas_call(
        functools.partial(_route_body, cap=cap),
        out_shape=[jax.ShapeDtypeStruct(shape, jnp.int32), jax.ShapeDtypeStruct(shape, F32)],
        scratch_shapes=[pltpu.VMEM(shape, F32)],
        compiler_params=pltpu.CompilerParams(vmem_limit_bytes=VMEM_LIMIT),
        name="route",
    )(afft.reshape(shape))


def _dispatch_body(off_ref, pos_ref, h_ref, xe_ref, acc, *, cap):
    e = pl.program_id(0)
    nb = pos_ref.shape[1]
    acc[...] = jnp.zeros_like(acc)
    r = lax.broadcasted_iota(jnp.int32, (DISPATCH_WIN, TOK_TILE), 0)

    def body(i, carry):
        start = jnp.minimum((off_ref[e, i] // SUBLANES) * SUBLANES, cap - DISPATCH_WIN)
        start = pl.multiple_of(start, SUBLANES)
        onehot = jnp.where(pos_ref[0, pl.ds(i, 1), :] - start == r, 1.0, 0.0).astype(BF16)
        xt = h_ref[pl.ds(pl.multiple_of(i * TOK_TILE, TOK_TILE), TOK_TILE), :]
        acc[pl.ds(start, DISPATCH_WIN), :] += _mm(onehot, xt)
        return carry

    lax.fori_loop(0, nb, body, 0)
    xe_ref[0] = acc[...].astype(BF16)


def _dispatch(off, pos, h2, cap):
    n, d = h2.shape
    nb = n // TOK_TILE
    return pl.pallas_call(
        functools.partial(_dispatch_body, cap=cap),
        grid_spec=pltpu.PrefetchScalarGridSpec(
            num_scalar_prefetch=1,
            grid=(N_EXPERTS,),
            in_specs=[
                pl.BlockSpec((1, nb, LANES), lambda e, off: (e, 0, 0)),
                pl.BlockSpec((n, d), lambda e, off: (0, 0), pipeline_mode=pl.Buffered(1)),
            ],
            out_specs=pl.BlockSpec((1, cap, d), lambda e, off: (e, 0, 0)),
            scratch_shapes=[pltpu.VMEM((cap, d), F32)],
        ),
        out_shape=jax.ShapeDtypeStruct((N_EXPERTS, cap, d), BF16),
        compiler_params=_cparams("arbitrary"),
        name="dispatch",
    )(off, pos, h2)


def _ffn_body(xe_ref, w1_ref, w3_ref, w2_ref, ye_ref):
    xe = xe_ref[0]
    a = _mm(xe, w1_ref[0].astype(BF16))
    b = _mm(xe, w3_ref[0].astype(BF16))
    hid = (a * jax.nn.sigmoid(a) * b).astype(BF16)
    ye_ref[0] = _mm(hid, w2_ref[0].astype(BF16)).astype(BF16)


def _ffn(xe, w1, w3, w2):
    e, cap, d = xe.shape
    f = w1.shape[2]
    return pl.pallas_call(
        _ffn_body,
        grid=(e,),
        in_specs=[
            pl.BlockSpec((1, cap, d), lambda i: (i, 0, 0)),
            pl.BlockSpec((1, d, f), lambda i: (i, 0, 0)),
            pl.BlockSpec((1, d, f), lambda i: (i, 0, 0)),
            pl.BlockSpec((1, f, d), lambda i: (i, 0, 0)),
        ],
        out_specs=pl.BlockSpec((1, cap, d), lambda i: (i, 0, 0)),
        out_shape=jax.ShapeDtypeStruct((e, cap, d), BF16),
        compiler_params=_cparams("parallel"),
        name="ffn",
    )(xe, w1, w3, w2)


def _combine_body(off_ref, pos_ref, x_ref, aff_ref, g2_ref, ye_ref, o_ref, *, cap):
    i = pl.program_id(0)
    r = lax.broadcasted_iota(jnp.int32, (COMBINE_WIN, TOK_TILE), 0)
    acc = jnp.zeros(x_ref.shape, F32)
    for e in range(N_EXPERTS):
        start = jnp.minimum((off_ref[e, i] // SUBLANES) * SUBLANES, cap - COMBINE_WIN)
        start = pl.multiple_of(start, SUBLANES)
        onehot = jnp.where(pos_ref[e] - start == r, 1.0, 0.0).astype(BF16)
        win = ye_ref[e, pl.ds(start, COMBINE_WIN), :]
        acc = acc + aff_ref[:, e:e + 1] * _mm(onehot, win, _TN)
    o_ref[...] = x_ref[...] + g2_ref[0] * acc


def _combine(off, pos, x, aff, gate2, ye, rows_per_mod):
    n, d = x.shape
    nb = n // TOK_TILE
    per = rows_per_mod // TOK_TILE
    cap = ye.shape[1]
    return pl.pallas_call(
        functools.partial(_combine_body, cap=cap),
        grid_spec=pltpu.PrefetchScalarGridSpec(
            num_scalar_prefetch=1,
            grid=(nb,),
            in_specs=[
                pl.BlockSpec((N_EXPERTS, None, 1, LANES), lambda i, off: (0, i, 0, 0)),
                pl.BlockSpec((TOK_TILE, d), lambda i, off: (i, 0)),
                pl.BlockSpec((TOK_TILE, LANES), lambda i, off: (i, 0)),
                pl.BlockSpec((1, 1, d), lambda i, off: (i // per, 0, 0)),
                pl.BlockSpec(ye.shape, lambda i, off: (0, 0, 0), pipeline_mode=pl.Buffered(1)),
            ],
            out_specs=pl.BlockSpec((TOK_TILE, d), lambda i, off: (i, 0)),
        ),
        out_shape=jax.ShapeDtypeStruct((n, d), F32),
        compiler_params=_cparams("arbitrary"),
        name="combine",
    )(off, pos.reshape(N_EXPERTS, nb, 1, LANES), x, aff, gate2, ye)


def _block_diag(w):
    z = jnp.zeros_like(w[:, 0::2])
    top = jnp.concatenate([w[:, 0::2], z], axis=-1)
    bot = jnp.concatenate([z, w[:, 1::2]], axis=-1)
    return jnp.concatenate([top, bot], axis=-2)


def _rope_tables(n_tokens):
    rows = n_tokens // GRID_W
    row = jnp.repeat(jnp.arange(rows, dtype=F32), GRID_W)
    col = jnp.tile(jnp.arange(GRID_W, dtype=F32), rows)
    quarter = HEAD_DIM // 4
    inv_freq = ROPE_THETA ** (-jnp.arange(quarter, dtype=F32) / quarter)
    ang = jnp.concatenate([row[:, None] * inv_freq, col[:, None] * inv_freq], axis=-1)
    cos, sin = jnp.cos(ang), jnp.sin(ang)
    cos_l = jnp.tile(cos, (1, LANES // (HEAD_DIM // 2)))
    sin_l = jnp.tile(jnp.concatenate([-sin, sin], axis=-1), (1, LANES // HEAD_DIM))
    return cos_l, sin_l


def _moe(x_mid, h2, aff, afft, gate2, w1, w3, w2, rows_per_mod):
    n = x_mid.shape[0]
    cap = max(1, EC_CAPACITY_FACTOR * n // N_EXPERTS)
    pos, off = _route(afft, cap)
    off_i = off[:, :, 0].astype(jnp.int32)
    xe = _dispatch(off_i, pos, h2, cap)
    ye = _ffn(xe, w1, w3, w2)
    return _combine(off_i, pos, x_mid, aff, gate2, ye, rows_per_mod)


def kernel(x_prompt, x_sample, state_lru, cache_diff_k, cache_diff_v, state_delta, cache_gqa_k, cache_gqa_v,
           c, c_ctx, norm1_g, norm2_g, w_mod, b_mod, w_in_ab, lru_conv_w, lru_conv_b, lru_wa, lru_ba,
           lru_wi, lru_bi, lru_lam, diff_q_gain, diff_k_gain, diff_lam, diff_sub_gain, w_in_cd,
           delta_conv_w, delta_a_log, delta_dt_bias, delta_norm_g, gqa_q_gain, gqa_k_gain, w_out,
           router_w, exp_w1, exp_w3, exp_w2):
    d = D_MODEL
    dec_batch, dec_seq = x_sample.shape[:2]
    ctx_batch, ctx_seq = x_prompt.shape[:2]
    n_ctx_groups = c.shape[0]

    cs = jnp.concatenate([c_ctx[None, :], c, jnp.zeros((SUBLANES - 1 - n_ctx_groups, d), F32)], axis=0)
    mod = _mod_call(cs, w_mod, b_mod)

    r = jnp.arange(LANES)
    group_mean = jnp.where((r[:, None] // HEAD_DIM) == (r[None, :] // HEAD_DIM), 1.0 / HEAD_DIM, 0.0).astype(BF16)
    rope = _rope_tables(dec_seq)
    lane_pad = lambda v, at: jnp.zeros((1, LANES), F32).at[0, at:at + v.size].set(v.reshape(-1))

    def run_group(x3, mod_rows, ctx):
        batch, seq = x3.shape[:2]
        n = batch * seq
        x = x3.reshape(n, d)
        is_ctx = ctx is None
        news = []
        for l in range(DEPTH):
            li = l // 2
            m6 = mod[l, mod_rows[0]:mod_rows[1]].reshape(-1, 6, 1, d)
            shift1, scale1, gate1, shift2, scale2, gate2 = (m6[:, k] for k in range(6))
            g1 = norm1_g[l].reshape(1, d)
            if l % 2 == 0:
                z = _inproj(x, g1, scale1, shift1, w_in_ab[li].astype(BF16), seq if not is_ctx else n)
                h0 = jnp.zeros((batch, 2, LRU_WIDTH), F32) if is_ctx else state_lru[:, li]
                ya, lru_state = _lru(z, batch, lru_conv_w[li], lru_conv_b[li].reshape(1, -1),
                                     _block_diag(lru_wa[li]), lru_ba[li], _block_diag(lru_wi[li]), lru_bi[li],
                                     lru_lam[li], h0)
                gq = jnp.tile(diff_q_gain[li], 512 // HEAD_DIM).reshape(1, -1)
                gk = jnp.tile(diff_k_gain[li], 512 // HEAD_DIM).reshape(1, -1)
                v = z[:, 2048:2560]
                if is_ctx:
                    qn, kc, kb = _prep(z, 2, 3, 512, gq, gk, group_mean, None, seq)
                    kk = kb.reshape(batch, seq, 512)
                    vv = v.astype(BF16).reshape(batch, seq, 512)
                    news.append((lru_state, kc.reshape(batch, seq, DIFF_HEADS, 2, HEAD_DIM),
                                 v.reshape(batch, seq, DIFF_HEADS, 2 * HEAD_DIM)))
                else:
                    qn, kb = _prep(z, 2, 3, 512, gq, gk, group_mean, rope, seq)
                    past = cache_diff_k.shape[2]
                    kk = jnp.concatenate([cache_diff_k[:, li].reshape(batch, past, 512).astype(BF16),
                                          kb.reshape(batch, seq, 512)], axis=1)
                    vv = jnp.concatenate([cache_diff_v[:, li].reshape(batch, past, 512).astype(BF16),
                                          v.astype(BF16).reshape(batch, seq, 512)], axis=1)
                lam_init = 0.8 - 0.6 * math.exp(-0.3 * l)
                yb = _attn(qn, kk, vv, diff_lam[li], diff_sub_gain[li].reshape(1, -1), batch, True, lam_init)
            else:
                w = w_in_cd[li]
                w_p = jnp.concatenate([w[:, :2048], w[:, 2064:2832], w[:, 2048:2064],
                                       jnp.zeros((d, IN_CD_PAD - 2832), F32)], axis=1).astype(BF16)
                z = _inproj(x, g1, scale1, shift1, w_p, seq if not is_ctx else n)
                s0 = (jnp.zeros((batch, 2, DELTA_HEADS, DELTA_DK, DELTA_DK), F32) if is_ctx
                      else state_delta[:, li])
                ya, delta_state = _delta(z, batch, delta_conv_w[li], lane_pad(delta_a_log[li], 8),
                                         lane_pad(delta_dt_bias[li], 8), delta_norm_g[li].reshape(1, -1), s0)
                gq = jnp.tile(gqa_q_gain[li], 512 // HEAD_DIM).reshape(1, -1)
                gk = jnp.tile(gqa_k_gain[li], LANES // HEAD_DIM).reshape(1, -1)
                v = z[:, 2688:2816]
                dup = lambda a: jnp.repeat(a.reshape(a.shape[0], a.shape[1], GQA_KV_HEADS, 1, HEAD_DIM), 2,
                                           axis=3).reshape(a.shape[0], a.shape[1], 2 * LANES)
                if is_ctx:
                    qn, kc, kb = _prep(z, 4, 20, LANES, gq, gk, group_mean, None, seq)
                    kk = dup(kb.reshape(batch, seq, LANES))
                    vv = dup(v.astype(BF16).reshape(batch, seq, LANES))
                    news.append((delta_state, kc.reshape(batch, seq, GQA_KV_HEADS, HEAD_DIM),
                                 v.reshape(batch, seq, GQA_KV_HEADS, HEAD_DIM)))
                else:
                    qn, kb = _prep(z, 4, 20, LANES, gq, gk, group_mean, rope, seq)
                    past = cache_gqa_k.shape[2]
                    kk = dup(jnp.concatenate([cache_gqa_k[:, li].reshape(batch, past, LANES).astype(BF16),
                                              kb.reshape(batch, seq, LANES)], axis=1))
                    vv = dup(jnp.concatenate([cache_gqa_v[:, li].reshape(batch, past, LANES).astype(BF16),
                                              v.astype(BF16).reshape(batch, seq, LANES)], axis=1))
                yb = _attn(qn, kk, vv, jnp.zeros((4, HEAD_DIM), F32), jnp.zeros((1, LANES), F32), batch, False, 0.0)
            wo = w_out[l].astype(BF16)
            rw = jnp.zeros((d, LANES), F32).at[:, :N_EXPERTS].set(router_w[l])
            x_mid, h2, aff, afft = _outproj(x, ya, yb, wo[:512], wo[512:], gate1, norm2_g[l].reshape(1, d),
                                            scale2, shift2, rw, router_w[l].T, seq if not is_ctx else n)
            x = _moe(x_mid, h2, aff, afft, gate2, exp_w1[l], exp_w3[l], exp_w2[l], seq if not is_ctx else n)
        return x.reshape(batch, seq, d), news

    y_prompt, news = run_group(x_prompt, (0, 1), None)
    y_sample, _ = run_group(x_sample, (1, 1 + dec_batch), True)

    dtype = x_prompt.dtype
    even, odd = news[0::2], news[1::2]
    stack = lambda items, k: jnp.stack([it[k] for it in items], axis=1).astype(dtype)
    return (y_prompt, y_sample, stack(even, 0), stack(even, 1), stack(even, 2),
            stack(odd, 0), stack(odd, 1), stack(odd, 2))
```

```python
import functools
import math

import jax
import jax.numpy as jnp
from jax import lax
from jax.experimental import pallas as pl
from jax.experimental.pallas import tpu as pltpu

F32 = jnp.float32
BF16 = jnp.bfloat16

D_MODEL = 1024
DEPTH = 4
GRID_W = 64
HEAD_DIM = 64
ROPE_THETA = 10000.0
RMS_EPS = 1e-6
LRU_WIDTH = 512
LRU_BLOCKS = 8
LRU_C = 8.0
DIFF_HEADS = 4
DELTA_HEADS = 4
DELTA_DK = 128
GQA_HEADS = 8
GQA_KV_HEADS = 2
N_EXPERTS = 16
EC_CAPACITY_FACTOR = 2
EXPERT_FF = 512
IN_AB = 2560
IN_CD_PAD = 2944

LANES = 128
SUBLANES = 8
VMEM_LIMIT = 56 * 1024 * 1024

Q_SCALE = HEAD_DIM ** -0.5 * math.log2(math.e)
CHUNK = 128
TOK_TILE = 128
DISPATCH_EXPERTS = 4
SMALL_COUNT = 25
SMALL_WIN = SMALL_COUNT + 2 * SUBLANES - 1
BIG_WIN = TOK_TILE + 2 * SUBLANES
COMBINE_SMALL_WIN = 48


def _cparams(*sem):
    return pltpu.CompilerParams(dimension_semantics=sem, vmem_limit_bytes=VMEM_LIMIT)


def _mm(a, b, dims=(((1,), (0,)), ((), ()))):
    return lax.dot_general(a, b, dims, preferred_element_type=F32)


_NT = (((1,), (1,)), ((), ()))
_TN = (((0,), (0,)), ((), ()))


def _dot1(a, b, dims=(((1,), (0,)), ((), ()))):
    return _mm(a.astype(BF16), b.astype(BF16), dims)


def _split(x):
    hi = x.astype(BF16)
    lo = (x - hi.astype(F32)).astype(BF16)
    return hi, lo


def _dot3(a, b, dims=(((1,), (0,)), ((), ()))):
    ah, al = _split(a)
    bh, bl = _split(b)
    return _mm(ah, bh, dims) + (_mm(ah, bl, dims) + _mm(al, bh, dims))


def _dot_exact_rhs(m_bf16, x):
    x1 = x.astype(BF16)
    r1 = x - x1.astype(F32)
    x2 = r1.astype(BF16)
    x3 = (r1 - x2.astype(F32)).astype(BF16)
    return _mm(m_bf16, x1) + (_mm(m_bf16, x2) + _mm(m_bf16, x3))


def _dot_exact_lhs(x, m_bf16):
    x1 = x.astype(BF16)
    r1 = x - x1.astype(F32)
    x2 = r1.astype(BF16)
    x3 = (r1 - x2.astype(F32)).astype(BF16)
    return _mm(x1, m_bf16) + (_mm(x2, m_bf16) + _mm(x3, m_bf16))


def _rms(x, g):
    return x * lax.rsqrt(jnp.mean(x * x, axis=-1, keepdims=True) + RMS_EPS) * g


def _mod_body(c_ref, w_ref, b_ref, o_ref):
    c = c_ref[...]
    a = c * jax.nn.sigmoid(c)
    o_ref[0] = _dot3(a, w_ref[0]) + b_ref[0]


def _mod_call(cs, w_mod, b_mod):
    rows, d = cs.shape
    width = w_mod.shape[2]
    tn = 512
    return pl.pallas_call(
        _mod_body,
        grid=(DEPTH, width // tn),
        in_specs=[
            pl.BlockSpec((rows, d), lambda l, j: (0, 0)),
            pl.BlockSpec((1, d, tn), lambda l, j: (l, 0, j)),
            pl.BlockSpec((1, 1, tn), lambda l, j: (l, 0, j)),
        ],
        out_specs=pl.BlockSpec((1, rows, tn), lambda l, j: (l, 0, j)),
        out_shape=jax.ShapeDtypeStruct((DEPTH, rows, width), F32),
        compiler_params=_cparams("parallel", "parallel"),
        name="mod",
    )(cs, w_mod, b_mod.reshape(DEPTH, 1, width))


def _inproj_body(x_ref, g_ref, sc_ref, sh_ref, w_ref, z_ref):
    h = _rms(x_ref[...], g_ref[...]) * (1.0 + sc_ref[0]) + sh_ref[0]
    z_ref[...] = _mm(h.astype(BF16), w_ref[...])


def _inproj(x, g, scale, shift, w, rows_per_mod):
    n, d = x.shape
    width = w.shape[1]
    tm = 256
    per = rows_per_mod // tm
    return pl.pallas_call(
        _inproj_body,
        grid=(n // tm,),
        in_specs=[
            pl.BlockSpec((tm, d), lambda i: (i, 0)),
            pl.BlockSpec((1, d), lambda i: (0, 0)),
            pl.BlockSpec((1, 1, d), lambda i: (i // per, 0, 0)),
            pl.BlockSpec((1, 1, d), lambda i: (i // per, 0, 0)),
            pl.BlockSpec((d, width), lambda i: (0, 0)),
        ],
        out_specs=pl.BlockSpec((tm, width), lambda i: (i, 0)),
        out_shape=jax.ShapeDtypeStruct((n, width), F32),
        compiler_params=_cparams("parallel"),
        name="inproj",
    )(x, g, scale, shift, w)


def _shift_rows(x, k, t_iota, seq):
    rows = x.shape[0]
    if k > 0:
        return jnp.where(t_iota >= k, pltpu.roll(x, k, 0), 0.0)
    return jnp.where(t_iota < seq + k, pltpu.roll(x, rows + k, 0), 0.0)


def _conv4(x, w_ref, t_iota, seq):
    return (
        w_ref[0:1, :] * _shift_rows(x, 2, t_iota, seq)
        + w_ref[1:2, :] * _shift_rows(x, 1, t_iota, seq)
        + w_ref[2:3, :] * x
        + w_ref[3:4, :] * _shift_rows(x, -1, t_iota, seq)
    )


def _tile_scan(a, b, row, reverse):
    for d in (1, 2, 4):
        if reverse:
            ok = row < SUBLANES - d
            a_sh = pltpu.roll(a, SUBLANES - d, 0)
            b_sh = pltpu.roll(b, SUBLANES - d, 0)
        else:
            ok = row >= d
            a_sh = pltpu.roll(a, d, 0)
            b_sh = pltpu.roll(b, d, 0)
        a_sh = jnp.where(ok, a_sh, 1.0)
        b_sh = jnp.where(ok, b_sh, 0.0)
        b = a * b_sh + b
        a = a * a_sh
    return a, b


def _lru_body(xa_ref, ga_ref, cw_ref, cb_ref, wa_ref, ba_ref, wi_ref, bi_ref, lam_ref, h0_ref,
              y_ref, st_ref, af, bf, ab, bb, hf, hb, *, seq):
    rows = xa_ref.shape[0]
    n_seq = rows // seq
    t_iota = lax.broadcasted_iota(jnp.int32, (rows, LANES), 0) % seq
    u = _conv4(xa_ref[...], cw_ref, t_iota, seq) + cb_ref[...]
    ub = u.astype(BF16)
    for d, (a_s, b_s) in enumerate(((af, bf), (ab, bb))):
        r = jax.nn.sigmoid(_mm(ub, wa_ref[d].astype(BF16)) + ba_ref[d:d + 1, :])
        i = jax.nn.sigmoid(_mm(ub, wi_ref[d].astype(BF16)) + bi_ref[d:d + 1, :])
        log_a = -LRU_C * r * jax.nn.softplus(-lam_ref[d:d + 1, :])
        a = jnp.exp(log_a)
        a_s[...] = a
        b_s[...] = jnp.sqrt(1.0 - a * a) * (i * u)

    n8 = seq // SUBLANES
    row = lax.broadcasted_iota(jnp.int32, (SUBLANES, LANES), 0)

    def step(i, carry):
        out = []
        for b in range(n_seq):
            h_f, h_b = carry[2 * b], carry[2 * b + 1]
            r0 = pl.multiple_of(b * seq + i * SUBLANES, SUBLANES)
            a_c, b_c = _tile_scan(af[pl.ds(r0, SUBLANES), :], bf[pl.ds(r0, SUBLANES), :], row, False)
            hf_t = a_c * h_f + b_c
            hf[pl.ds(r0, SUBLANES), :] = hf_t
            r1 = pl.multiple_of(b * seq + (n8 - 1 - i) * SUBLANES, SUBLANES)
            a_c, b_c = _tile_scan(ab[pl.ds(r1, SUBLANES), :], bb[pl.ds(r1, SUBLANES), :], row, True)
            hb_t = a_c * h_b + b_c
            hb[pl.ds(r1, SUBLANES), :] = hb_t
            out += [hf_t[SUBLANES - 1:SUBLANES, :], hb_t[0:1, :]]
        return tuple(out)

    init = tuple(h0_ref[b, d:d + 1, :] for b in range(n_seq) for d in range(2))
    final = lax.fori_loop(0, n8, step, init, unroll=2)
    y_ref[...] = ((hf[...] + hb[...]) * jax.nn.gelu(ga_ref[...])).astype(BF16)
    for b in range(n_seq):
        st_ref[b, 0:1, :] = final[2 * b]
        st_ref[b, 1:2, :] = final[2 * b + 1]


LRU_BLOCK_ROWS = 1024


def _lru(z, batch, conv_w, conv_b, wa_bd, ba, wi_bd, bi, lam, h0):
    n = z.shape[0]
    seq = n // batch
    nb = max(1, LRU_BLOCK_ROWS // seq)
    t = nb * seq
    nj = LRU_WIDTH // LANES
    return pl.pallas_call(
        functools.partial(_lru_body, seq=seq),
        grid=(batch // nb, nj),
        in_specs=[
            pl.BlockSpec((t, LANES), lambda b, j: (b, j)),
            pl.BlockSpec((t, LANES), lambda b, j: (b, nj + j)),
            pl.BlockSpec((4, LANES), lambda b, j: (0, j)),
            pl.BlockSpec((1, LANES), lambda b, j: (0, j)),
            pl.BlockSpec((2, None, LANES, LANES), lambda b, j: (0, j, 0, 0)),
            pl.BlockSpec((2, LANES), lambda b, j: (0, j)),
            pl.BlockSpec((2, None, LANES, LANES), lambda b, j: (0, j, 0, 0)),
            pl.BlockSpec((2, LANES), lambda b, j: (0, j)),
            pl.BlockSpec((2, LANES), lambda b, j: (0, j)),
            pl.BlockSpec((nb, 2, LANES), lambda b, j: (b, 0, j)),
        ],
        out_specs=[
            pl.BlockSpec((t, LANES), lambda b, j: (b, j)),
            pl.BlockSpec((nb, 2, LANES), lambda b, j: (b, 0, j)),
        ],
        out_shape=[
            jax.ShapeDtypeStruct((n, LRU_WIDTH), BF16),
            jax.ShapeDtypeStruct((batch, 2, LRU_WIDTH), F32),
        ],
        scratch_shapes=[pltpu.VMEM((t, LANES), F32)] * 6,
        compiler_params=_cparams("parallel", "parallel"),
        name="lru",
    )(z, z, conv_w, conv_b, wa_bd, ba, wi_bd, bi, lam, h0)


def _head_norm(x, g_ref, gm_ref):
    outs = []
    for j in range(x.shape[1] // LANES):
        xs = x[:, j * LANES:(j + 1) * LANES]
        sq = xs * xs
        hi, lo = _split(sq)
        ms = _mm(hi, gm_ref[...]) + _mm(lo, gm_ref[...])
        outs.append(xs * lax.rsqrt(ms + RMS_EPS) * g_ref[:, j * LANES:(j + 1) * LANES])
    return outs


def _rope(xs, cos, sin, first_half):
    swapped = jnp.where(first_half, pltpu.roll(xs, LANES - HEAD_DIM // 2, 1), pltpu.roll(xs, HEAD_DIM // 2, 1))
    return xs * cos + swapped * sin


def _prep_ctx_body(q_ref, k_ref, gq_ref, gk_ref, gm_ref, qo_ref, kc_ref, kb_ref):
    qs = _head_norm(q_ref[...], gq_ref, gm_ref)
    ks = _head_norm(k_ref[...], gk_ref, gm_ref)
    for j, xs in enumerate(qs):
        qo_ref[:, j * LANES:(j + 1) * LANES] = (xs * Q_SCALE).astype(BF16)
    for j, xs in enumerate(ks):
        kc_ref[:, j * LANES:(j + 1) * LANES] = xs
        kb_ref[:, j * LANES:(j + 1) * LANES] = xs.astype(BF16)


def _prep_rope_body(q_ref, k_ref, gq_ref, gk_ref, gm_ref, cos_ref, sin_ref, qo_ref, kb_ref):
    qs = _head_norm(q_ref[...], gq_ref, gm_ref)
    ks = _head_norm(k_ref[...], gk_ref, gm_ref)
    cos = cos_ref[...]
    sin = sin_ref[...]
    lane = lax.broadcasted_iota(jnp.int32, cos.shape, 1)
    first_half = (lane % HEAD_DIM) < HEAD_DIM // 2
    for j, xs in enumerate(qs):
        qo_ref[:, j * LANES:(j + 1) * LANES] = (_rope(xs, cos, sin, first_half) * Q_SCALE).astype(BF16)
    for j, xs in enumerate(ks):
        kb_ref[:, j * LANES:(j + 1) * LANES] = _rope(xs, cos, sin, first_half).astype(BF16)


def _prep(z, q_blk, k_blk, kw, gq, gk, gm, rope, seq):
    n = z.shape[0]
    tm = 256
    qw = 512
    in_specs = [
        pl.BlockSpec((tm, qw), lambda i: (i, q_blk)),
        pl.BlockSpec((tm, kw), lambda i: (i, k_blk)),
        pl.BlockSpec((1, qw), lambda i: (0, 0)),
        pl.BlockSpec((1, kw), lambda i: (0, 0)),
        pl.BlockSpec((LANES, LANES), lambda i: (0, 0)),
    ]
    q_out = (pl.BlockSpec((tm, qw), lambda i: (i, 0)), jax.ShapeDtypeStruct((n, qw), BF16))
    kb_out = (pl.BlockSpec((tm, kw), lambda i: (i, 0)), jax.ShapeDtypeStruct((n, kw), BF16))
    if rope is None:
        kc_out = (pl.BlockSpec((tm, kw), lambda i: (i, 0)), jax.ShapeDtypeStruct((n, kw), F32))
        outs = (q_out, kc_out, kb_out)
        body = _prep_ctx_body
        args = (z, z, gq, gk, gm)
    else:
        per = seq // tm
        in_specs += [pl.BlockSpec((tm, LANES), lambda i: (i % per, 0))] * 2
        outs = (q_out, kb_out)
        body = _prep_rope_body
        args = (z, z, gq, gk, gm, rope[0], rope[1])
    return pl.pallas_call(
        body,
        grid=(n // tm,),
        in_specs=in_specs,
        out_specs=[o[0] for o in outs],
        out_shape=[o[1] for o in outs],
        compiler_params=_cparams("parallel"),
        name="prep",
    )(*args)


def _attn_body(q_ref, k_ref, v_ref, lam_ref, sg_ref, o_ref, *, diff, lam_init, tk, share):
    for j in range(q_ref.shape[1] // LANES):
        q_cols = slice(j * LANES, (j + 1) * LANES)
        kv_cols = slice((j // share) * LANES, (j // share + 1) * LANES)
        o_ref[:, q_cols] = _attn_pair(q_ref[:, q_cols], k_ref, v_ref, kv_cols, lam_ref, sg_ref,
                                      diff=diff, lam_init=lam_init, tk=tk)


def _attn_pair(q, k_ref, v_ref, kv_cols, lam_ref, sg_ref, *, diff, lam_init, tk):
    tq = q.shape[0]
    lane = lax.broadcasted_iota(jnp.int32, (tq, LANES), 1)
    zero = jnp.zeros_like(q)
    q2 = jnp.concatenate([jnp.where(lane < HEAD_DIM, q, zero), jnp.where(lane >= HEAD_DIM, q, zero)], axis=0)
    nkb = k_ref.shape[1] // tk

    def scores(kb):
        r0 = pl.multiple_of(kb * tk, tk)
        return _mm(q2, k_ref[0, pl.ds(r0, tk), kv_cols], _NT), r0

    def max_blk(kb, rm):
        s, _ = scores(kb)
        for c in range(tk // LANES):
            rm = jnp.maximum(rm, s[:, c * LANES:(c + 1) * LANES])
        return rm

    rm = lax.fori_loop(0, nkb, max_blk, jnp.full((2 * tq, LANES), -jnp.inf, F32), unroll=True)
    m = jnp.max(rm, axis=-1, keepdims=True)

    ones = jnp.ones((tk, LANES), BF16)

    def acc_blk(kb, carry):
        l, acc = carry
        s, r0 = scores(kb)
        p = jnp.exp2(s - m).astype(BF16)
        return l + _mm(p, ones), acc + _mm(p, v_ref[0, pl.ds(r0, tk), kv_cols])

    zeros = jnp.zeros((2 * tq, LANES), F32)
    l, acc = lax.fori_loop(0, nkb, acc_blk, (zeros, zeros), unroll=True)
    o2 = acc / l
    o0 = o2[:tq]
    o1 = o2[tq:]
    if diff:
        lv = lam_ref[...]
        lam = (jnp.exp(jnp.sum(lv[0:1] * lv[1:2], axis=-1, keepdims=True))
               - jnp.exp(jnp.sum(lv[2:3] * lv[3:4], axis=-1, keepdims=True)) + lam_init)
        y = _rms(o0 - lam * o1, sg_ref[...]) * (1.0 - lam_init)
    else:
        y = jnp.where(lane < HEAD_DIM, o0, o1)
    return y.astype(BF16)


ATTN_SHORT_KEYS = 1024


def _attn(qn, kk, vv, lam_p, sub_gain, batch, diff, lam_init):
    n, qw = qn.shape
    t = n // batch
    tq = 256
    nq = t // tq
    tkv = kk.shape[1]
    tk = min(512, tkv)
    q_blocks = qw // LANES
    kv_blocks = kk.shape[2] // LANES
    share = q_blocks // kv_blocks
    per_step = q_blocks if tkv <= ATTN_SHORT_KEYS else 1
    kv_per_step = max(1, per_step // share)
    kv_map = (lambda b, j, i: (b, 0, j)) if per_step > 1 else (lambda b, j, i: (b, 0, j // share))
    return pl.pallas_call(
        functools.partial(_attn_body, diff=diff, lam_init=lam_init, tk=tk, share=share if per_step > 1 else 1),
        grid=(batch, q_blocks // per_step, nq),
        in_specs=[
            pl.BlockSpec((tq, per_step * LANES), lambda b, j, i: (b * nq + i, j)),
            pl.BlockSpec((1, tkv, kv_per_step * LANES), kv_map),
            pl.BlockSpec((1, tkv, kv_per_step * LANES), kv_map),
            pl.BlockSpec((4, HEAD_DIM), lambda b, j, i: (0, 0)),
            pl.BlockSpec((1, LANES), lambda b, j, i: (0, 0)),
        ],
        out_specs=pl.BlockSpec((tq, per_step * LANES), lambda b, j, i: (b * nq + i, j)),
        out_shape=jax.ShapeDtypeStruct((n, qw), BF16),
        compiler_params=_cparams("parallel", "parallel", "parallel"),
        name="attn",
    )(qn, kk, vv, lam_p, sub_gain)


INV_BASE = 8


_BNN = (((2,), (1,)), ((0,), (0,)))
_BNT = (((2,), (2,)), ((0,), (0,)))


def _bdot3(a, b):
    ah, al = _split(a)
    bh, bl = _split(b)
    return _mm(ah, bh, _BNN) + (_mm(ah, bl, _BNN) + _mm(al, bh, _BNN))


def _unit_tri_inverse(a, eye, blocks):
    same_base, level_masks = blocks
    d = jnp.where(same_base, a, 0.0)
    p = eye - d
    q = _bdot3(d, d)
    p = p + _bdot3(p, q)
    q = _bdot3(q, q)
    x = p + _bdot3(p, q)
    for m in level_masks:
        x = x - _bdot3(_bdot3(x, jnp.where(m, a, 0.0)), x)
    return x


def _delta_prepare(qc, kc, vc, cc, n_fwd, consts):
    incl, strict, cum_m, eye, blocks = consts
    x1 = cc.astype(BF16)
    r1 = cc - x1.astype(F32)
    x2 = r1.astype(BF16)
    x3 = (r1 - x2.astype(F32)).astype(BF16)
    cum = _mm(cum_m, x1, _BNN) + (_mm(cum_m, x2, _BNN) + _mm(cum_m, x3, _BNN))
    cum_t = jnp.swapaxes(cum, 1, 2)
    g_col = cum[:, :, 2:3]
    g_row = cum_t[:, 2:3, :]
    beta = cc[:, :, 0:1]
    g_last = jnp.concatenate([g_row[:n_fwd, :, CHUNK - 1:CHUNK], g_row[n_fwd:, :, 0:1]], axis=0)
    decay = jnp.exp(jnp.where(incl > 0.0, g_col - g_row, -jnp.inf))
    kb = kc * beta
    kcb = kc.astype(BF16)
    a = jnp.where(strict > 0.0, _mm(kb.astype(BF16), kcb, _BNT) * decay, 0.0)
    inv = _unit_tri_inverse(a, eye, blocks)
    e_g = jnp.exp(g_col)
    sol = _bdot3(inv, jnp.concatenate([kb * e_g, vc * beta], axis=2))
    w = sol[:, :, :DELTA_DK]
    u = sol[:, :, DELTA_DK:]
    qk = jnp.where(incl > 0.0, _mm(qc.astype(BF16), kcb, _BNT) * decay, 0.0)
    k_tail = kc * jnp.exp(g_last - g_col)
    return (w.astype(BF16), u, qk.astype(BF16), (qc * e_g).astype(BF16), k_tail.astype(BF16), jnp.exp(g_last))


def _delta_apply(s, pre, g):
    w, u, qk, q_dec, k_tail, chunk_decay = (t[g] for t in pre)
    sb = s.astype(BF16)
    v_new = u - _mm(w, sb)
    vb = v_new.astype(BF16)
    o = _mm(q_dec, sb) + _mm(qk, vb)
    return o, s * chunk_decay + _mm(k_tail, vb, _TN)


def _delta_body(q_ref, k_ref, v_ref, gate_ref, lg_ref, cwq_ref, cwk_ref, cwv_ref, alog_ref, dtb_ref, ng_ref,
                s0_ref, o_ref, st_ref, qs, ks, vs, cs_f, cs_b, o_f, o_b, *, seq, group):
    rows = q_ref.shape[0]
    head = pl.program_id(1)
    t_iota = lax.broadcasted_iota(jnp.int32, (rows, LANES), 0) % seq
    lane = lax.broadcasted_iota(jnp.int32, (rows, LANES), 1)

    def conv_silu(ref, cw):
        x = _conv4(ref[...], cw, t_iota, seq)
        return x * jax.nn.sigmoid(x)

    q = conv_silu(q_ref, cwq_ref)
    k = conv_silu(k_ref, cwk_ref)
    qs[...] = q * lax.rsqrt(jnp.sum(q * q, axis=-1, keepdims=True) + 1e-6) * DELTA_DK ** -0.5
    ks[...] = k * lax.rsqrt(jnp.sum(k * k, axis=-1, keepdims=True) + 1e-6)
    vs[...] = conv_silu(v_ref, cwv_ref)
    lg = lg_ref[...]
    beta = jax.nn.sigmoid(lg)
    g_all = -jnp.exp(alog_ref[...]) * jax.nn.softplus(lg + dtb_ref[...])

    def col(arr, idx):
        return jnp.sum(jnp.where(lane == idx, arr, 0.0), axis=1, keepdims=True)

    cs_f[...] = jnp.where(lane == 0, col(beta, head), jnp.where(lane == 2, col(g_all, 2 * DELTA_HEADS + head), 0.0))
    cs_b[...] = jnp.where(lane == 0, col(beta, DELTA_HEADS + head),
                          jnp.where(lane == 2, col(g_all, 3 * DELTA_HEADS + head), 0.0))

    n_seq = rows // seq
    half = n_seq * group
    r = lax.broadcasted_iota(jnp.int32, (CHUNK, CHUNK), 0)
    c = lax.broadcasted_iota(jnp.int32, (CHUNK, CHUNK), 1)
    eye = jnp.where(r == c, 1.0, 0.0)
    level_masks = []
    size = INV_BASE
    while size < CHUNK:
        level_masks.append(((r // (2 * size)) == (c // (2 * size))) & ((r // size) != (c // size)))
        size *= 2
    blocks = ((r // INV_BASE) == (c // INV_BASE), level_masks)

    def per_direction(fwd, bwd):
        stack = lambda m: jnp.broadcast_to(jnp.where(m, 1.0, 0.0)[None], (half, CHUNK, CHUNK))
        return jnp.concatenate([stack(fwd), stack(bwd)], axis=0)

    incl = per_direction(r >= c, r <= c)
    consts = (incl, per_direction(r > c, r < c), incl.astype(BF16), eye, blocks)
    n = seq // CHUNK

    def chunk_rows(r0):
        return pl.ds(r0 if isinstance(r0, int) else pl.multiple_of(r0, CHUNK), CHUNK)

    def trip(i, states):
        rows_f = [[b * seq + (i * group + g) * CHUNK for g in range(group)] for b in range(n_seq)]
        rows_b = [[b * seq + (n - 1 - (i * group + g)) * CHUNK for g in range(group)] for b in range(n_seq)]
        chains = [(r0, cs_f) for rb in rows_f for r0 in rb] + [(r0, cs_b) for rb in rows_b for r0 in rb]
        load = lambda ref: jnp.stack([ref[chunk_rows(r0), :] for r0, _ in chains], axis=0)
        cc = jnp.stack([cref[chunk_rows(r0), :] for r0, cref in chains], axis=0)
        pre = _delta_prepare(load(qs), load(ks), load(vs), cc, half, consts)
        out = []
        for b in range(n_seq):
            s_f, s_b = states[2 * b], states[2 * b + 1]
            for g in range(group):
                o, s_f = _delta_apply(s_f, pre, b * group + g)
                o_f[chunk_rows(rows_f[b][g]), :] = o
                o, s_b = _delta_apply(s_b, pre, half + b * group + g)
                o_b[chunk_rows(rows_b[b][g]), :] = o
            out += [s_f, s_b]
        return tuple(out)

    init = tuple(s0_ref[b, d, 0] for b in range(n_seq) for d in range(2))
    final = trip(0, init) if n == group else lax.fori_loop(0, n // group, trip, init)
    for b in range(n_seq):
        st_ref[b, 0, 0] = final[2 * b]
        st_ref[b, 1, 0] = final[2 * b + 1]
    gate = gate_ref[...]
    o_ref[...] = (_rms(o_f[...] + o_b[...], ng_ref[...]) * (gate * jax.nn.sigmoid(gate))).astype(BF16)


DELTA_GROUP = 4
DELTA_BLOCK_ROWS = 1024


def _delta(z, batch, conv_w, alog_l, dtb_l, norm_g, s0):
    n = z.shape[0]
    t = n // batch
    h = DELTA_HEADS
    nb = max(1, DELTA_BLOCK_ROWS // t)
    rows = nb * t
    group = min(DELTA_GROUP, t // CHUNK)
    blk = lambda off: pl.BlockSpec((rows, LANES), lambda b, j: (b, off + j))
    cwb = lambda off: pl.BlockSpec((4, LANES), lambda b, j: (0, off + j))
    lg_blk = 22
    vec = pl.BlockSpec((1, LANES), lambda b, j: (0, 0))
    st_spec = pl.BlockSpec((nb, 2, 1, DELTA_DK, DELTA_DK), lambda b, j: (b, 0, j, 0, 0))
    return pl.pallas_call(
        functools.partial(_delta_body, seq=t, group=group),
        grid=(batch // nb, h),
        in_specs=[blk(0), blk(h), blk(2 * h), blk(3 * h), pl.BlockSpec((rows, LANES), lambda b, j: (b, lg_blk)),
                  cwb(0), cwb(h), cwb(2 * h), vec, vec, vec, st_spec],
        out_specs=[pl.BlockSpec((rows, LANES), lambda b, j: (b, j)), st_spec],
        out_shape=[jax.ShapeDtypeStruct((n, h * LANES), BF16),
                   jax.ShapeDtypeStruct((batch, 2, h, DELTA_DK, DELTA_DK), F32)],
        scratch_shapes=[pltpu.VMEM((rows, LANES), F32)] * 7,
        compiler_params=_cparams("parallel", "parallel"),
        name="delta",
    )(z, z, z, z, z, conv_w, conv_w, conv_w, alog_l, dtb_l, norm_g, s0)


def _outproj_body(x_ref, ya_ref, yb_ref, wa_ref, wb_ref, g1_ref, n2_ref, sc_ref, sh_ref, rwt_ref,
                  xo_ref, h2_ref, afft_ref):
    y = _mm(ya_ref[...], wa_ref[...]) + _mm(yb_ref[...], wb_ref[...])
    x = x_ref[...] + g1_ref[0] * y
    xo_ref[...] = x
    h = _rms(x, n2_ref[...]) * (1.0 + sc_ref[0]) + sh_ref[0]
    h2_ref[...] = h.astype(BF16)
    lt = _dot3(rwt_ref[...], h, _NT)
    et = jnp.exp(lt - jnp.max(lt, axis=0, keepdims=True))
    afft_ref[...] = et / jnp.sum(et, axis=0, keepdims=True)


def _outproj(x, ya, yb, wa, wb, gate1, n2, scale2, shift2, rwt, rows_per_mod):
    n, d = x.shape
    tm = 256
    per = rows_per_mod // tm
    half = ya.shape[1]
    row = lambda w: pl.BlockSpec((tm, w), lambda i: (i, 0))
    mod = pl.BlockSpec((1, 1, d), lambda i: (i // per, 0, 0))
    full = lambda a: pl.BlockSpec(a.shape, lambda i: (0,) * a.ndim)
    return pl.pallas_call(
        _outproj_body,
        grid=(n // tm,),
        in_specs=[row(d), row(half), row(half), full(wa), full(wb), mod, full(n2), mod, mod, full(rwt)],
        out_specs=[row(d), row(d), pl.BlockSpec((N_EXPERTS, tm), lambda i: (0, i))],
        out_shape=[jax.ShapeDtypeStruct((n, d), F32), jax.ShapeDtypeStruct((n, d), BF16),
                   jax.ShapeDtypeStruct((N_EXPERTS, n), F32)],
        compiler_params=_cparams("parallel"),
        name="outproj",
    )(x, ya, yb, wa, wb, gate1, n2, scale2, shift2, rwt)


def _route_body(aff_ref, pos_ref, off_ref, boff, *, cap):
    nb = aff_ref.shape[1]
    aff = aff_ref[...]

    def count(mask):
        c = jnp.sum(jnp.where(mask, 1.0, 0.0), axis=2, keepdims=True)
        return jnp.sum(c, axis=1, keepdims=True)

    def as_float(bits):
        return lax.bitcast_convert_type(bits, F32)

    top_bit = 29

    def search(i, thr):
        cand = thr | jnp.left_shift(jnp.int32(1), top_bit - i)
        return jnp.where(count(aff >= as_float(cand)) >= cap, cand, thr)

    thr = lax.fori_loop(0, top_bit + 1, search, jnp.zeros((N_EXPERTS, 1, 1), jnp.int32))
    above = aff >= as_float(thr + 1)
    tied = (aff >= as_float(thr)) & jnp.logical_not(above)
    need = cap - count(above)

    r = lax.broadcasted_iota(jnp.int32, (LANES, LANES), 0)
    c = lax.broadcasted_iota(jnp.int32, (LANES, LANES), 1)
    before = jnp.where(r < c, 1.0, 0.0).astype(BF16)
    ones = jnp.ones((LANES, LANES), BF16)
    rb = lax.broadcasted_iota(jnp.int32, (nb, nb), 0)
    cb = lax.broadcasted_iota(jnp.int32, (nb, nb), 1)
    blocks_before = jnp.where(cb < rb, 1.0, 0.0).astype(BF16)

    def excl_cumsum(mask):
        m2 = jnp.where(mask, 1.0, 0.0).astype(BF16).reshape(N_EXPERTS * nb, LANES)
        within = _mm(m2, before).reshape(N_EXPERTS, nb, LANES)
        tot = _mm(m2, ones).astype(BF16).reshape(N_EXPERTS, nb, LANES)
        for e in range(N_EXPERTS):
            boff[e] = _mm(blocks_before, tot[e])
        return within + boff[...]

    sel = above | (tied & (excl_cumsum(tied) < need))
    pos = excl_cumsum(sel)
    pos_ref[...] = jnp.where(sel, pos, -1.0e6).astype(jnp.int32)
    off_ref[...] = boff[...]


def _route(afft, cap):
    e, n = afft.shape
    nb = n // LANES
    shape = (e, nb, LANES)
    return pl.pallas_call(
        functools.partial(_route_body, cap=cap),
        out_shape=[jax.ShapeDtypeStruct(shape, jnp.int32), jax.ShapeDtypeStruct(shape, F32)],
        scratch_shapes=[pltpu.VMEM(shape, F32)],
        compiler_params=pltpu.CompilerParams(vmem_limit_bytes=VMEM_LIMIT),
        name="route",
    )(afft.reshape(shape))


def _tile_counts_small(off_ref, experts, i):
    small = None
    for e in experts:
        ok = off_ref[e, i + 1] - off_ref[e, i] <= SMALL_COUNT
        small = ok if small is None else jnp.logical_and(small, ok)
    return small


def _dispatch_body(off_ref, pos_ref, h_ref, xe_ref, acc, *, cap):
    g0 = pl.program_id(0) * DISPATCH_EXPERTS
    nb = pos_ref.shape[1]
    d = h_ref.shape[1]
    acc[:, 0:SUBLANES, :] = jnp.zeros((DISPATCH_EXPERTS, SUBLANES, d), F32)
    head_row = lax.broadcasted_iota(jnp.int32, (SUBLANES, d), 0)

    def scatter(i, win):
        xt = h_ref[pl.ds(pl.multiple_of(i * TOK_TILE, TOK_TILE), TOK_TILE), :]
        r = lax.broadcasted_iota(jnp.int32, (win, TOK_TILE), 0)
        for j in range(DISPATCH_EXPERTS):
            start = off_ref[g0 + j, i]
            base = pl.multiple_of((start // SUBLANES) * SUBLANES, SUBLANES)
            onehot = jnp.where(pos_ref[j, pl.ds(i, 1), :] - base == r, 1.0, 0.0).astype(BF16)
            rows = _mm(onehot, xt)
            head = pl.ds(base, SUBLANES)
            acc[j, head, :] = jnp.where(head_row >= start - base, rows[:SUBLANES], acc[j, head, :])
            acc[j, pl.ds(base + SUBLANES, win - SUBLANES), :] = rows[SUBLANES:]

    def body(i, carry):
        small = _tile_counts_small(off_ref, [g0 + j for j in range(DISPATCH_EXPERTS)], i)

        @pl.when(small)
        def _():
            scatter(i, SMALL_WIN)

        @pl.when(jnp.logical_not(small))
        def _():
            scatter(i, BIG_WIN)

        return carry

    lax.fori_loop(0, nb, body, 0)
    for j in range(DISPATCH_EXPERTS):
        xe_ref[j] = acc[j, 0:cap, :].astype(BF16)


def _dispatch(off, pos, h2, cap):
    n, d = h2.shape
    nb = n // TOK_TILE
    return pl.pallas_call(
        functools.partial(_dispatch_body, cap=cap),
        grid_spec=pltpu.PrefetchScalarGridSpec(
            num_scalar_prefetch=1,
            grid=(N_EXPERTS // DISPATCH_EXPERTS,),
            in_specs=[
                pl.BlockSpec((DISPATCH_EXPERTS, nb, LANES), lambda e, off: (e, 0, 0)),
                pl.BlockSpec((n, d), lambda e, off: (0, 0), pipeline_mode=pl.Buffered(1)),
            ],
            out_specs=pl.BlockSpec((DISPATCH_EXPERTS, cap, d), lambda e, off: (e, 0, 0)),
            scratch_shapes=[pltpu.VMEM((DISPATCH_EXPERTS, cap + BIG_WIN, d), F32)],
        ),
        out_shape=jax.ShapeDtypeStruct((N_EXPERTS, cap, d), BF16),
        compiler_params=_cparams("arbitrary"),
        name="dispatch",
    )(off, pos, h2)


def _ffn_body(xe_ref, w1_ref, w3_ref, w2_ref, ye_ref):
    xe = xe_ref[0]
    a = _mm(xe, w1_ref[0].astype(BF16))
    b = _mm(xe, w3_ref[0].astype(BF16))
    hid = (a * jax.nn.sigmoid(a) * b).astype(BF16)
    ye_ref[0] = _mm(hid, w2_ref[0].astype(BF16)).astype(BF16)


def _ffn(xe, w1, w3, w2):
    e, cap, d = xe.shape
    f = w1.shape[2]
    return pl.pallas_call(
        _ffn_body,
        grid=(e,),
        in_specs=[
            pl.BlockSpec((1, cap, d), lambda i: (i, 0, 0)),
            pl.BlockSpec((1, d, f), lambda i: (i, 0, 0)),
            pl.BlockSpec((1, d, f), lambda i: (i, 0, 0)),
            pl.BlockSpec((1, f, d), lambda i: (i, 0, 0)),
        ],
        out_specs=pl.BlockSpec((1, cap, d), lambda i: (i, 0, 0)),
        out_shape=jax.ShapeDtypeStruct((e, cap, d), BF16),
        compiler_params=_cparams("parallel"),
        name="ffn",
    )(xe, w1, w3, w2)


def _combine_body(off_ref, pos_ref, x_ref, afft_ref, g2_ref, ye_ref, o_ref, *, cap):
    i = pl.program_id(0)

    def gather(win):
        r = lax.broadcasted_iota(jnp.int32, (win, TOK_TILE), 0)
        hi, lo, wins = [], [], []
        for e in range(N_EXPERTS):
            start = jnp.minimum((off_ref[e, i] // SUBLANES) * SUBLANES, cap - win)
            start = pl.multiple_of(start, SUBLANES)
            gated = jnp.where(pos_ref[e] - start == r, afft_ref[e:e + 1, :], 0.0)
            g_hi, g_lo = _split(gated)
            hi.append(g_hi)
            lo.append(g_lo)
            wins.append(ye_ref[e, pl.ds(start, win), :])
        gates = jnp.concatenate([jnp.concatenate(hi, axis=0), jnp.concatenate(lo, axis=0)], axis=1)
        both = _mm(gates, jnp.concatenate(wins, axis=0), _TN)
        o_ref[...] = x_ref[...] + g2_ref[0] * (both[:TOK_TILE] + both[TOK_TILE:])

    small = None
    for e in range(N_EXPERTS):
        ok = off_ref[e, i + 1] - off_ref[e, i] <= COMBINE_SMALL_WIN - SUBLANES + 1
        small = ok if small is None else jnp.logical_and(small, ok)

    @pl.when(small)
    def _():
        gather(COMBINE_SMALL_WIN)

    @pl.when(jnp.logical_not(small))
    def _():
        gather(BIG_WIN)


def _combine(off, pos, x, afft, gate2, ye, rows_per_mod):
    n, d = x.shape
    nb = n // TOK_TILE
    per = rows_per_mod // TOK_TILE
    cap = ye.shape[1]
    return pl.pallas_call(
        functools.partial(_combine_body, cap=cap),
        grid_spec=pltpu.PrefetchScalarGridSpec(
            num_scalar_prefetch=1,
            grid=(nb,),
            in_specs=[
                pl.BlockSpec((N_EXPERTS, None, 1, LANES), lambda i, off: (0, i, 0, 0)),
                pl.BlockSpec((TOK_TILE, d), lambda i, off: (i, 0)),
                pl.BlockSpec((N_EXPERTS, TOK_TILE), lambda i, off: (0, i)),
                pl.BlockSpec((1, 1, d), lambda i, off: (i // per, 0, 0)),
                pl.BlockSpec(ye.shape, lambda i, off: (0, 0, 0), pipeline_mode=pl.Buffered(1)),
            ],
            out_specs=pl.BlockSpec((TOK_TILE, d), lambda i, off: (i, 0)),
        ),
        out_shape=jax.ShapeDtypeStruct((n, d), F32),
        compiler_params=_cparams("arbitrary"),
        name="combine",
    )(off, pos.reshape(N_EXPERTS, nb, 1, LANES), x, afft, gate2, ye)


def _block_diag(w):
    z = jnp.zeros_like(w[:, 0::2])
    top = jnp.concatenate([w[:, 0::2], z], axis=-1)
    bot = jnp.concatenate([z, w[:, 1::2]], axis=-1)
    return jnp.concatenate([top, bot], axis=-2)


def _rope_tables(n_tokens):
    rows = n_tokens // GRID_W
    row = jnp.repeat(jnp.arange(rows, dtype=F32), GRID_W)
    col = jnp.tile(jnp.arange(GRID_W, dtype=F32), rows)
    quarter = HEAD_DIM // 4
    inv_freq = ROPE_THETA ** (-jnp.arange(quarter, dtype=F32) / quarter)
    ang = jnp.concatenate([row[:, None] * inv_freq, col[:, None] * inv_freq], axis=-1)
    cos, sin = jnp.cos(ang), jnp.sin(ang)
    cos_l = jnp.tile(cos, (1, LANES // (HEAD_DIM // 2)))
    sin_l = jnp.tile(jnp.concatenate([-sin, sin], axis=-1), (1, LANES // HEAD_DIM))
    return cos_l, sin_l


def _moe(x_mid, h2, afft, gate2, w1, w3, w2, rows_per_mod):
    n = x_mid.shape[0]
    cap = max(1, EC_CAPACITY_FACTOR * n // N_EXPERTS)
    pos, off = _route(afft, cap)
    off_i = jnp.concatenate([off[:, :, 0].astype(jnp.int32), jnp.full((N_EXPERTS, 1), cap, jnp.int32)], axis=1)
    xe = _dispatch(off_i, pos, h2, cap)
    ye = _ffn(xe, w1, w3, w2)
    return _combine(off_i, pos, x_mid, afft, gate2, ye, rows_per_mod)


def kernel(x_prompt, x_sample, state_lru, cache_diff_k, cache_diff_v, state_delta, cache_gqa_k, cache_gqa_v,
           c, c_ctx, norm1_g, norm2_g, w_mod, b_mod, w_in_ab, lru_conv_w, lru_conv_b, lru_wa, lru_ba,
           lru_wi, lru_bi, lru_lam, diff_q_gain, diff_k_gain, diff_lam, diff_sub_gain, w_in_cd,
           delta_conv_w, delta_a_log, delta_dt_bias, delta_norm_g, gqa_q_gain, gqa_k_gain, w_out,
           router_w, exp_w1, exp_w3, exp_w2):
    d = D_MODEL
    dec_batch, dec_seq = x_sample.shape[:2]
    ctx_batch, ctx_seq = x_prompt.shape[:2]
    n_ctx_groups = c.shape[0]

    cs = jnp.concatenate([c_ctx[None, :], c, jnp.zeros((SUBLANES - 1 - n_ctx_groups, d), F32)], axis=0)
    mod = _mod_call(cs, w_mod, b_mod)

    r = jnp.arange(LANES)
    group_mean = jnp.where((r[:, None] // HEAD_DIM) == (r[None, :] // HEAD_DIM), 1.0 / HEAD_DIM, 0.0).astype(BF16)
    rope = _rope_tables(dec_seq)
    lane_pad = lambda v, at: jnp.zeros((1, LANES), F32).at[0, at:at + v.size].set(v.reshape(-1))

    def run_group(x3, mod_rows, ctx):
        batch, seq = x3.shape[:2]
        n = batch * seq
        x = x3.reshape(n, d)
        is_ctx = ctx is None
        news = []
        for l in range(DEPTH):
            li = l // 2
            m6 = mod[l, mod_rows[0]:mod_rows[1]].reshape(-1, 6, 1, d)
            shift1, scale1, gate1, shift2, scale2, gate2 = (m6[:, k] for k in range(6))
            g1 = norm1_g[l].reshape(1, d)
            if l % 2 == 0:
                z = _inproj(x, g1, scale1, shift1, w_in_ab[li].astype(BF16), seq if not is_ctx else n)
                h0 = jnp.zeros((batch, 2, LRU_WIDTH), F32) if is_ctx else state_lru[:, li]
                ya, lru_state = _lru(z, batch, lru_conv_w[li], lru_conv_b[li].reshape(1, -1),
                                     _block_diag(lru_wa[li]), lru_ba[li], _block_diag(lru_wi[li]), lru_bi[li],
                                     lru_lam[li], h0)
                gq = jnp.tile(diff_q_gain[li], 512 // HEAD_DIM).reshape(1, -1)
                gk = jnp.tile(diff_k_gain[li], 512 // HEAD_DIM).reshape(1, -1)
                v = z[:, 2048:2560]
                if is_ctx:
                    qn, kc, kb = _prep(z, 2, 3, 512, gq, gk, group_mean, None, seq)
                    kk = kb.reshape(batch, seq, 512)
                    vv = v.astype(BF16).reshape(batch, seq, 512)
                    news.append((lru_state, kc.reshape(batch, seq, DIFF_HEADS, 2, HEAD_DIM),
                                 v.reshape(batch, seq, DIFF_HEADS, 2 * HEAD_DIM)))
                else:
                    qn, kb = _prep(z, 2, 3, 512, gq, gk, group_mean, rope, seq)
                    past = cache_diff_k.shape[2]
                    kk = jnp.concatenate([cache_diff_k[:, li].reshape(batch, past, 512).astype(BF16),
                                          kb.reshape(batch, seq, 512)], axis=1)
                    vv = jnp.concatenate([cache_diff_v[:, li].reshape(batch, past, 512).astype(BF16),
                                          v.astype(BF16).reshape(batch, seq, 512)], axis=1)
                lam_init = 0.8 - 0.6 * math.exp(-0.3 * l)
                yb = _attn(qn, kk, vv, diff_lam[li], diff_sub_gain[li].reshape(1, -1), batch, True, lam_init)
            else:
                w = w_in_cd[li]
                w_p = jnp.concatenate([w[:, :2048], w[:, 2064:2832], w[:, 2048:2064],
                                       jnp.zeros((d, IN_CD_PAD - 2832), F32)], axis=1).astype(BF16)
                z = _inproj(x, g1, scale1, shift1, w_p, seq if not is_ctx else n)
                s0 = (jnp.zeros((batch, 2, DELTA_HEADS, DELTA_DK, DELTA_DK), F32) if is_ctx
                      else state_delta[:, li])
                ya, delta_state = _delta(z, batch, delta_conv_w[li], lane_pad(delta_a_log[li], 8),
                                         lane_pad(delta_dt_bias[li], 8), delta_norm_g[li].reshape(1, -1), s0)
                gq = jnp.tile(gqa_q_gain[li], 512 // HEAD_DIM).reshape(1, -1)
                gk = jnp.tile(gqa_k_gain[li], LANES // HEAD_DIM).reshape(1, -1)
                v = z[:, 2688:2816]
                dup = lambda a: jnp.repeat(a.reshape(a.shape[0], a.shape[1], GQA_KV_HEADS, 1, HEAD_DIM), 2,
                                           axis=3).reshape(a.shape[0], a.shape[1], 2 * LANES)
                if is_ctx:
                    qn, kc, kb = _prep(z, 4, 20, LANES, gq, gk, group_mean, None, seq)
                    kk = dup(kb.reshape(batch, seq, LANES))
                    vv = dup(v.astype(BF16).reshape(batch, seq, LANES))
                    news.append((delta_state, kc.reshape(batch, seq, GQA_KV_HEADS, HEAD_DIM),
                                 v.reshape(batch, seq, GQA_KV_HEADS, HEAD_DIM)))
                else:
                    qn, kb = _prep(z, 4, 20, LANES, gq, gk, group_mean, rope, seq)
                    past = cache_gqa_k.shape[2]
                    kk = dup(jnp.concatenate([cache_gqa_k[:, li].reshape(batch, past, LANES).astype(BF16),
                                              kb.reshape(batch, seq, LANES)], axis=1))
                    vv = dup(jnp.concatenate([cache_gqa_v[:, li].reshape(batch, past, LANES).astype(BF16),
                                              v.astype(BF16).reshape(batch, seq, LANES)], axis=1))
                yb = _attn(qn, kk, vv, jnp.zeros((4, HEAD_DIM), F32), jnp.zeros((1, LANES), F32), batch, False, 0.0)
            wo = w_out[l].astype(BF16)
            x_mid, h2, afft = _outproj(x, ya, yb, wo[:512], wo[512:], gate1, norm2_g[l].reshape(1, d),
                                       scale2, shift2, router_w[l].T, seq if not is_ctx else n)
            x = _moe(x_mid, h2, afft, gate2, exp_w1[l], exp_w3[l], exp_w2[l], seq if not is_ctx else n)
        return x.reshape(batch, seq, d), news

    y_prompt, news = run_group(x_prompt, (0, 1), None)
    y_sample, _ = run_group(x_sample, (1, 1 + dec_batch), True)

    dtype = x_prompt.dtype
    even, odd = news[0::2], news[1::2]
    stack = lambda items, k: jnp.stack([it[k] for it in items], axis=1).astype(dtype)
    return (y_prompt, y_sample, stack(even, 0), stack(even, 1), stack(even, 2),
            stack(odd, 0), stack(odd, 1), stack(odd, 2))
```

```python
import functools
import math

import jax
import jax.numpy as jnp
from jax import lax
from jax.experimental import pallas as pl
from jax.experimental.pallas import tpu as pltpu

F32 = jnp.float32
BF16 = jnp.bfloat16

D_MODEL = 1024
DEPTH = 4
GRID_W = 64
HEAD_DIM = 64
ROPE_THETA = 10000.0
RMS_EPS = 1e-6
LRU_WIDTH = 512
LRU_BLOCKS = 8
LRU_C = 8.0
DIFF_HEADS = 4
DELTA_HEADS = 4
DELTA_DK = 128
GQA_HEADS = 8
GQA_KV_HEADS = 2
N_EXPERTS = 16
EC_CAPACITY_FACTOR = 2
EXPERT_FF = 512
IN_AB = 2560
IN_CD_PAD = 2944

LANES = 128
SUBLANES = 8
VMEM_LIMIT = 56 * 1024 * 1024

Q_SCALE = HEAD_DIM ** -0.5 * math.log2(math.e)
CHUNK = 128
TOK_TILE = 128
DISPATCH_EXPERTS = 4
SMALL_COUNT = 25
SMALL_WIN = SMALL_COUNT + 2 * SUBLANES - 1
BIG_WIN = TOK_TILE + 2 * SUBLANES
COMBINE_SMALL_WIN = SMALL_COUNT + SUBLANES - 1


def _cparams(*sem):
    return pltpu.CompilerParams(dimension_semantics=sem, vmem_limit_bytes=VMEM_LIMIT)


def _mm(a, b, dims=(((1,), (0,)), ((), ()))):
    return lax.dot_general(a, b, dims, preferred_element_type=F32)


_NT = (((1,), (1,)), ((), ()))
_TN = (((0,), (0,)), ((), ()))


def _dot1(a, b, dims=(((1,), (0,)), ((), ()))):
    return _mm(a.astype(BF16), b.astype(BF16), dims)


def _split(x):
    hi = x.astype(BF16)
    lo = (x - hi.astype(F32)).astype(BF16)
    return hi, lo


def _dot3(a, b, dims=(((1,), (0,)), ((), ()))):
    ah, al = _split(a)
    bh, bl = _split(b)
    return _mm(ah, bh, dims) + (_mm(ah, bl, dims) + _mm(al, bh, dims))


def _dot_exact_rhs(m_bf16, x):
    x1 = x.astype(BF16)
    r1 = x - x1.astype(F32)
    x2 = r1.astype(BF16)
    x3 = (r1 - x2.astype(F32)).astype(BF16)
    return _mm(m_bf16, x1) + (_mm(m_bf16, x2) + _mm(m_bf16, x3))


def _dot_exact_lhs(x, m_bf16):
    x1 = x.astype(BF16)
    r1 = x - x1.astype(F32)
    x2 = r1.astype(BF16)
    x3 = (r1 - x2.astype(F32)).astype(BF16)
    return _mm(x1, m_bf16) + (_mm(x2, m_bf16) + _mm(x3, m_bf16))


def _rms(x, g):
    return x * lax.rsqrt(jnp.mean(x * x, axis=-1, keepdims=True) + RMS_EPS) * g


def _mod_body(c_ref, w_ref, b_ref, o_ref):
    c = c_ref[...]
    a = c * jax.nn.sigmoid(c)
    o_ref[0] = _dot3(a, w_ref[0]) + b_ref[0]


def _mod_call(cs, w_mod, b_mod):
    rows, d = cs.shape
    width = w_mod.shape[2]
    tn = 512
    return pl.pallas_call(
        _mod_body,
        grid=(DEPTH, width // tn),
        in_specs=[
            pl.BlockSpec((rows, d), lambda l, j: (0, 0)),
            pl.BlockSpec((1, d, tn), lambda l, j: (l, 0, j)),
            pl.BlockSpec((1, 1, tn), lambda l, j: (l, 0, j)),
        ],
        out_specs=pl.BlockSpec((1, rows, tn), lambda l, j: (l, 0, j)),
        out_shape=jax.ShapeDtypeStruct((DEPTH, rows, width), F32),
        compiler_params=_cparams("parallel", "parallel"),
        name="mod",
    )(cs, w_mod, b_mod.reshape(DEPTH, 1, width))


def _inproj_body(x_ref, g_ref, sc_ref, sh_ref, w_ref, z_ref):
    h = _rms(x_ref[...], g_ref[...]) * (1.0 + sc_ref[0]) + sh_ref[0]
    z_ref[...] = _mm(h.astype(BF16), w_ref[...])


def _inproj(x, g, scale, shift, w, rows_per_mod):
    n, d = x.shape
    width = w.shape[1]
    tm = 256
    per = rows_per_mod // tm
    return pl.pallas_call(
        _inproj_body,
        grid=(n // tm,),
        in_specs=[
            pl.BlockSpec((tm, d), lambda i: (i, 0)),
            pl.BlockSpec((1, d), lambda i: (0, 0)),
            pl.BlockSpec((1, 1, d), lambda i: (i // per, 0, 0)),
            pl.BlockSpec((1, 1, d), lambda i: (i // per, 0, 0)),
            pl.BlockSpec((d, width), lambda i: (0, 0)),
        ],
        out_specs=pl.BlockSpec((tm, width), lambda i: (i, 0)),
        out_shape=jax.ShapeDtypeStruct((n, width), F32),
        compiler_params=_cparams("parallel"),
        name="inproj",
    )(x, g, scale, shift, w)


def _shift_rows(x, k, t_iota, seq):
    rows = x.shape[0]
    if k > 0:
        return jnp.where(t_iota >= k, pltpu.roll(x, k, 0), 0.0)
    return jnp.where(t_iota < seq + k, pltpu.roll(x, rows + k, 0), 0.0)


def _conv4(x, w_ref, t_iota, seq):
    return (
        w_ref[0:1, :] * _shift_rows(x, 2, t_iota, seq)
        + w_ref[1:2, :] * _shift_rows(x, 1, t_iota, seq)
        + w_ref[2:3, :] * x
        + w_ref[3:4, :] * _shift_rows(x, -1, t_iota, seq)
    )


def _tile_scan(a, b, row, reverse):
    for d in (1, 2, 4):
        if reverse:
            ok = row < SUBLANES - d
            a_sh = pltpu.roll(a, SUBLANES - d, 0)
            b_sh = pltpu.roll(b, SUBLANES - d, 0)
        else:
            ok = row >= d
            a_sh = pltpu.roll(a, d, 0)
            b_sh = pltpu.roll(b, d, 0)
        a_sh = jnp.where(ok, a_sh, 1.0)
        b_sh = jnp.where(ok, b_sh, 0.0)
        b = a * b_sh + b
        a = a * a_sh
    return a, b


def _lru_body(xa_ref, ga_ref, cw_ref, cb_ref, wa_ref, ba_ref, wi_ref, bi_ref, lam_ref, h0_ref,
              y_ref, st_ref, af, bf, ab, bb, hf, hb, *, seq):
    rows = xa_ref.shape[0]
    n_seq = rows // seq
    t_iota = lax.broadcasted_iota(jnp.int32, (rows, LANES), 0) % seq
    u = _conv4(xa_ref[...], cw_ref, t_iota, seq) + cb_ref[...]
    ub = u.astype(BF16)
    for d, (a_s, b_s) in enumerate(((af, bf), (ab, bb))):
        r = jax.nn.sigmoid(_mm(ub, wa_ref[d].astype(BF16)) + ba_ref[d:d + 1, :])
        i = jax.nn.sigmoid(_mm(ub, wi_ref[d].astype(BF16)) + bi_ref[d:d + 1, :])
        log_a = -LRU_C * r * jax.nn.softplus(-lam_ref[d:d + 1, :])
        a = jnp.exp(log_a)
        a_s[...] = a
        b_s[...] = jnp.sqrt(1.0 - a * a) * (i * u)

    n8 = seq // SUBLANES
    row = lax.broadcasted_iota(jnp.int32, (SUBLANES, LANES), 0)

    def step(i, carry):
        out = []
        for b in range(n_seq):
            h_f, h_b = carry[2 * b], carry[2 * b + 1]
            r0 = pl.multiple_of(b * seq + i * SUBLANES, SUBLANES)
            a_c, b_c = _tile_scan(af[pl.ds(r0, SUBLANES), :], bf[pl.ds(r0, SUBLANES), :], row, False)
            hf_t = a_c * h_f + b_c
            hf[pl.ds(r0, SUBLANES), :] = hf_t
            r1 = pl.multiple_of(b * seq + (n8 - 1 - i) * SUBLANES, SUBLANES)
            a_c, b_c = _tile_scan(ab[pl.ds(r1, SUBLANES), :], bb[pl.ds(r1, SUBLANES), :], row, True)
            hb_t = a_c * h_b + b_c
            hb[pl.ds(r1, SUBLANES), :] = hb_t
            out += [hf_t[SUBLANES - 1:SUBLANES, :], hb_t[0:1, :]]
        return tuple(out)

    init = tuple(h0_ref[b, d:d + 1, :] for b in range(n_seq) for d in range(2))
    final = lax.fori_loop(0, n8, step, init, unroll=2)
    y_ref[...] = ((hf[...] + hb[...]) * jax.nn.gelu(ga_ref[...])).astype(BF16)
    for b in range(n_seq):
        st_ref[b, 0:1, :] = final[2 * b]
        st_ref[b, 1:2, :] = final[2 * b + 1]


LRU_BLOCK_ROWS = 1024


def _lru(z, batch, conv_w, conv_b, wa_bd, ba, wi_bd, bi, lam, h0):
    n = z.shape[0]
    seq = n // batch
    nb = max(1, LRU_BLOCK_ROWS // seq)
    t = nb * seq
    nj = LRU_WIDTH // LANES
    return pl.pallas_call(
        functools.partial(_lru_body, seq=seq),
        grid=(batch // nb, nj),
        in_specs=[
            pl.BlockSpec((t, LANES), lambda b, j: (b, j)),
            pl.BlockSpec((t, LANES), lambda b, j: (b, nj + j)),
            pl.BlockSpec((4, LANES), lambda b, j: (0, j)),
            pl.BlockSpec((1, LANES), lambda b, j: (0, j)),
            pl.BlockSpec((2, None, LANES, LANES), lambda b, j: (0, j, 0, 0)),
            pl.BlockSpec((2, LANES), lambda b, j: (0, j)),
            pl.BlockSpec((2, None, LANES, LANES), lambda b, j: (0, j, 0, 0)),
            pl.BlockSpec((2, LANES), lambda b, j: (0, j)),
            pl.BlockSpec((2, LANES), lambda b, j: (0, j)),
            pl.BlockSpec((nb, 2, LANES), lambda b, j: (b, 0, j)),
        ],
        out_specs=[
            pl.BlockSpec((t, LANES), lambda b, j: (b, j)),
            pl.BlockSpec((nb, 2, LANES), lambda b, j: (b, 0, j)),
        ],
        out_shape=[
            jax.ShapeDtypeStruct((n, LRU_WIDTH), BF16),
            jax.ShapeDtypeStruct((batch, 2, LRU_WIDTH), F32),
        ],
        scratch_shapes=[pltpu.VMEM((t, LANES), F32)] * 6,
        compiler_params=_cparams("parallel", "parallel"),
        name="lru",
    )(z, z, conv_w, conv_b, wa_bd, ba, wi_bd, bi, lam, h0)


def _head_norm(x, g_ref, gm_ref):
    outs = []
    for j in range(x.shape[1] // LANES):
        xs = x[:, j * LANES:(j + 1) * LANES]
        sq = xs * xs
        hi, lo = _split(sq)
        ms = _mm(hi, gm_ref[...]) + _mm(lo, gm_ref[...])
        outs.append(xs * lax.rsqrt(ms + RMS_EPS) * g_ref[:, j * LANES:(j + 1) * LANES])
    return outs


def _rope(xs, cos, sin, first_half):
    swapped = jnp.where(first_half, pltpu.roll(xs, LANES - HEAD_DIM // 2, 1), pltpu.roll(xs, HEAD_DIM // 2, 1))
    return xs * cos + swapped * sin


def _prep_ctx_body(q_ref, k_ref, gq_ref, gk_ref, gm_ref, qo_ref, kc_ref, kb_ref):
    qs = _head_norm(q_ref[...], gq_ref, gm_ref)
    ks = _head_norm(k_ref[...], gk_ref, gm_ref)
    for j, xs in enumerate(qs):
        qo_ref[:, j * LANES:(j + 1) * LANES] = (xs * Q_SCALE).astype(BF16)
    for j, xs in enumerate(ks):
        kc_ref[:, j * LANES:(j + 1) * LANES] = xs
        kb_ref[:, j * LANES:(j + 1) * LANES] = xs.astype(BF16)


def _prep_rope_body(q_ref, k_ref, gq_ref, gk_ref, gm_ref, cos_ref, sin_ref, qo_ref, kb_ref):
    qs = _head_norm(q_ref[...], gq_ref, gm_ref)
    ks = _head_norm(k_ref[...], gk_ref, gm_ref)
    cos = cos_ref[...]
    sin = sin_ref[...]
    lane = lax.broadcasted_iota(jnp.int32, cos.shape, 1)
    first_half = (lane % HEAD_DIM) < HEAD_DIM // 2
    for j, xs in enumerate(qs):
        qo_ref[:, j * LANES:(j + 1) * LANES] = (_rope(xs, cos, sin, first_half) * Q_SCALE).astype(BF16)
    for j, xs in enumerate(ks):
        kb_ref[:, j * LANES:(j + 1) * LANES] = _rope(xs, cos, sin, first_half).astype(BF16)


def _prep(z, q_blk, k_blk, kw, gq, gk, gm, rope, seq):
    n = z.shape[0]
    tm = 256
    qw = 512
    in_specs = [
        pl.BlockSpec((tm, qw), lambda i: (i, q_blk)),
        pl.BlockSpec((tm, kw), lambda i: (i, k_blk)),
        pl.BlockSpec((1, qw), lambda i: (0, 0)),
        pl.BlockSpec((1, kw), lambda i: (0, 0)),
        pl.BlockSpec((LANES, LANES), lambda i: (0, 0)),
    ]
    q_out = (pl.BlockSpec((tm, qw), lambda i: (i, 0)), jax.ShapeDtypeStruct((n, qw), BF16))
    kb_out = (pl.BlockSpec((tm, kw), lambda i: (i, 0)), jax.ShapeDtypeStruct((n, kw), BF16))
    if rope is None:
        kc_out = (pl.BlockSpec((tm, kw), lambda i: (i, 0)), jax.ShapeDtypeStruct((n, kw), F32))
        outs = (q_out, kc_out, kb_out)
        body = _prep_ctx_body
        args = (z, z, gq, gk, gm)
    else:
        per = seq // tm
        in_specs += [pl.BlockSpec((tm, LANES), lambda i: (i % per, 0))] * 2
        outs = (q_out, kb_out)
        body = _prep_rope_body
        args = (z, z, gq, gk, gm, rope[0], rope[1])
    return pl.pallas_call(
        body,
        grid=(n // tm,),
        in_specs=in_specs,
        out_specs=[o[0] for o in outs],
        out_shape=[o[1] for o in outs],
        compiler_params=_cparams("parallel"),
        name="prep",
    )(*args)


def _attn_body(q_ref, k_ref, v_ref, lam_ref, sg_ref, o_ref, s_scr, *, diff, lam_init, tk, share):
    for j in range(q_ref.shape[1] // LANES):
        q_cols = slice(j * LANES, (j + 1) * LANES)
        kv_cols = slice((j // share) * LANES, (j // share + 1) * LANES)
        o_ref[:, q_cols] = _attn_pair(q_ref[:, q_cols], k_ref, v_ref, kv_cols, lam_ref, sg_ref, s_scr,
                                      diff=diff, lam_init=lam_init, tk=tk)


def _attn_pair(q, k_ref, v_ref, kv_cols, lam_ref, sg_ref, s_scr, *, diff, lam_init, tk):
    tq = q.shape[0]
    lane = lax.broadcasted_iota(jnp.int32, (tq, LANES), 1)
    zero = jnp.zeros_like(q)
    heads = (jnp.where(lane < HEAD_DIM, q, zero), jnp.where(lane >= HEAD_DIM, q, zero))
    nkb = k_ref.shape[1] // tk

    def max_blk(kb, rms):
        r0 = pl.multiple_of(kb * tk, tk)
        k = k_ref[0, pl.ds(r0, tk), kv_cols]
        out = []
        for h, (qh, rm) in enumerate(zip(heads, rms)):
            s = _mm(qh, k, _NT)
            s_scr[h, :, pl.ds(r0, tk)] = s
            for c in range(tk // LANES):
                rm = jnp.maximum(rm, s[:, c * LANES:(c + 1) * LANES])
            out.append(rm)
        return tuple(out)

    neg = jnp.full((tq, LANES), -jnp.inf, F32)
    rms = lax.fori_loop(0, nkb, max_blk, (neg, neg), unroll=True)
    ms = [jnp.max(rm, axis=-1, keepdims=True) for rm in rms]

    ones = jnp.ones((tk, LANES), BF16)

    def acc_blk(kb, accs):
        r0 = pl.multiple_of(kb * tk, tk)
        v1 = jnp.concatenate([v_ref[0, pl.ds(r0, tk), kv_cols], ones], axis=1)
        return tuple(acc + _mm(jnp.exp2(s_scr[h, :, pl.ds(r0, tk)] - m).astype(BF16), v1)
                     for h, (m, acc) in enumerate(zip(ms, accs)))

    zeros = jnp.zeros((tq, 2 * LANES), F32)
    accs = lax.fori_loop(0, nkb, acc_blk, (zeros, zeros), unroll=True)
    o0, o1 = (acc[:, :LANES] / acc[:, LANES:] for acc in accs)
    if diff:
        lv = lam_ref[...]
        lam = (jnp.exp(jnp.sum(lv[0:1] * lv[1:2], axis=-1, keepdims=True))
               - jnp.exp(jnp.sum(lv[2:3] * lv[3:4], axis=-1, keepdims=True)) + lam_init)
        y = _rms(o0 - lam * o1, sg_ref[...]) * (1.0 - lam_init)
    else:
        y = jnp.where(lane < HEAD_DIM, o0, o1)
    return y.astype(BF16)


ATTN_SHORT_KEYS = 1024


def _attn(qn, kk, vv, lam_p, sub_gain, batch, diff, lam_init):
    n, qw = qn.shape
    t = n // batch
    tq = 256
    nq = t // tq
    tkv = kk.shape[1]
    tk = min(512, tkv)
    q_blocks = qw // LANES
    kv_blocks = kk.shape[2] // LANES
    share = q_blocks // kv_blocks
    per_step = q_blocks if tkv <= ATTN_SHORT_KEYS else 1
    kv_per_step = max(1, per_step // share)
    kv_map = (lambda b, j, i: (b, 0, j)) if per_step > 1 else (lambda b, j, i: (b, 0, j // share))
    return pl.pallas_call(
        functools.partial(_attn_body, diff=diff, lam_init=lam_init, tk=tk, share=share if per_step > 1 else 1),
        grid=(batch, q_blocks // per_step, nq),
        in_specs=[
            pl.BlockSpec((tq, per_step * LANES), lambda b, j, i: (b * nq + i, j)),
            pl.BlockSpec((1, tkv, kv_per_step * LANES), kv_map),
            pl.BlockSpec((1, tkv, kv_per_step * LANES), kv_map),
            pl.BlockSpec((4, HEAD_DIM), lambda b, j, i: (0, 0)),
            pl.BlockSpec((1, LANES), lambda b, j, i: (0, 0)),
        ],
        out_specs=pl.BlockSpec((tq, per_step * LANES), lambda b, j, i: (b * nq + i, j)),
        out_shape=jax.ShapeDtypeStruct((n, qw), BF16),
        scratch_shapes=[pltpu.VMEM((2, tq, tkv), F32)],
        compiler_params=_cparams("parallel", "parallel", "parallel"),
        name="attn",
    )(qn, kk, vv, lam_p, sub_gain)


INV_BASE = 8


_BNN = (((2,), (1,)), ((0,), (0,)))
_BNT = (((2,), (2,)), ((0,), (0,)))


def _bdot3(a, b):
    ah, al = _split(a)
    bh, bl = _split(b)
    return _mm(ah, bh, _BNN) + (_mm(ah, bl, _BNN) + _mm(al, bh, _BNN))


def _unit_tri_inverse(a, eye, blocks):
    same_base, level_masks = blocks
    d = jnp.where(same_base, a, 0.0)
    p = eye - d
    q = _bdot3(d, d)
    p = p + _bdot3(p, q)
    q = _bdot3(q, q)
    x = p + _bdot3(p, q)
    for m in level_masks:
        x = x - _bdot3(_bdot3(x, jnp.where(m, a, 0.0)), x)
    return x


def _delta_prepare(qc, kc, vc, cc, n_fwd, consts):
    incl, strict, cum_m, eye, blocks = consts
    x1 = cc.astype(BF16)
    r1 = cc - x1.astype(F32)
    x2 = r1.astype(BF16)
    x3 = (r1 - x2.astype(F32)).astype(BF16)
    cum = _mm(cum_m, x1, _BNN) + (_mm(cum_m, x2, _BNN) + _mm(cum_m, x3, _BNN))
    cum_t = jnp.swapaxes(cum, 1, 2)
    g_col = cum[:, :, 2:3]
    g_row = cum_t[:, 2:3, :]
    beta = cc[:, :, 0:1]
    g_last = jnp.concatenate([g_row[:n_fwd, :, CHUNK - 1:CHUNK], g_row[n_fwd:, :, 0:1]], axis=0)
    decay = jnp.exp(jnp.where(incl > 0.0, g_col - g_row, -jnp.inf))
    kb = kc * beta
    kcb = kc.astype(BF16)
    a = jnp.where(strict > 0.0, _mm(kb.astype(BF16), kcb, _BNT) * decay, 0.0)
    inv = _unit_tri_inverse(a, eye, blocks)
    e_g = jnp.exp(g_col)
    sol = _bdot3(inv, jnp.concatenate([kb * e_g, vc * beta], axis=2))
    w = sol[:, :, :DELTA_DK]
    u = sol[:, :, DELTA_DK:]
    qk = jnp.where(incl > 0.0, _mm(qc.astype(BF16), kcb, _BNT) * decay, 0.0)
    k_tail = kc * jnp.exp(g_last - g_col)
    return (w.astype(BF16), u, qk.astype(BF16), (qc * e_g).astype(BF16), k_tail.astype(BF16), jnp.exp(g_last))


def _delta_apply(s, pre, g):
    w, u, qk, q_dec, k_tail, chunk_decay = (t[g] for t in pre)
    sb = s.astype(BF16)
    v_new = u - _mm(w, sb)
    vb = v_new.astype(BF16)
    o = _mm(q_dec, sb) + _mm(qk, vb)
    return o, s * chunk_decay + _mm(k_tail, vb, _TN)


def _delta_body(q_ref, k_ref, v_ref, gate_ref, lg_ref, cwq_ref, cwk_ref, cwv_ref, alog_ref, dtb_ref, ng_ref,
                s0_ref, o_ref, st_ref, qs, ks, vs, cs_f, cs_b, o_f, o_b, *, seq, group):
    rows = q_ref.shape[0]
    head = pl.program_id(1)
    t_iota = lax.broadcasted_iota(jnp.int32, (rows, LANES), 0) % seq
    lane = lax.broadcasted_iota(jnp.int32, (rows, LANES), 1)

    def conv_silu(ref, cw):
        x = _conv4(ref[...], cw, t_iota, seq)
        return x * jax.nn.sigmoid(x)

    q = conv_silu(q_ref, cwq_ref)
    k = conv_silu(k_ref, cwk_ref)
    qs[...] = q * lax.rsqrt(jnp.sum(q * q, axis=-1, keepdims=True) + 1e-6) * DELTA_DK ** -0.5
    ks[...] = k * lax.rsqrt(jnp.sum(k * k, axis=-1, keepdims=True) + 1e-6)
    vs[...] = conv_silu(v_ref, cwv_ref)
    lg = lg_ref[...]
    beta = jax.nn.sigmoid(lg)
    g_all = -jnp.exp(alog_ref[...]) * jax.nn.softplus(lg + dtb_ref[...])

    def col(arr, idx):
        return jnp.sum(jnp.where(lane == idx, arr, 0.0), axis=1, keepdims=True)

    cs_f[...] = jnp.where(lane == 0, col(beta, head), jnp.where(lane == 2, col(g_all, 2 * DELTA_HEADS + head), 0.0))
    cs_b[...] = jnp.where(lane == 0, col(beta, DELTA_HEADS + head),
                          jnp.where(lane == 2, col(g_all, 3 * DELTA_HEADS + head), 0.0))

    n_seq = rows // seq
    half = n_seq * group
    r = lax.broadcasted_iota(jnp.int32, (CHUNK, CHUNK), 0)
    c = lax.broadcasted_iota(jnp.int32, (CHUNK, CHUNK), 1)
    eye = jnp.where(r == c, 1.0, 0.0)
    level_masks = []
    size = INV_BASE
    while size < CHUNK:
        level_masks.append(((r // (2 * size)) == (c // (2 * size))) & ((r // size) != (c // size)))
        size *= 2
    blocks = ((r // INV_BASE) == (c // INV_BASE), level_masks)

    def per_direction(fwd, bwd):
        stack = lambda m: jnp.broadcast_to(jnp.where(m, 1.0, 0.0)[None], (half, CHUNK, CHUNK))
        return jnp.concatenate([stack(fwd), stack(bwd)], axis=0)

    incl = per_direction(r >= c, r <= c)
    consts = (incl, per_direction(r > c, r < c), incl.astype(BF16), eye, blocks)
    n = seq // CHUNK

    def chunk_rows(r0):
        return pl.ds(r0 if isinstance(r0, int) else pl.multiple_of(r0, CHUNK), CHUNK)

    def trip(i, states):
        rows_f = [[b * seq + (i * group + g) * CHUNK for g in range(group)] for b in range(n_seq)]
        rows_b = [[b * seq + (n - 1 - (i * group + g)) * CHUNK for g in range(group)] for b in range(n_seq)]
        chains = [(r0, cs_f) for rb in rows_f for r0 in rb] + [(r0, cs_b) for rb in rows_b for r0 in rb]
        load = lambda ref: jnp.stack([ref[chunk_rows(r0), :] for r0, _ in chains], axis=0)
        cc = jnp.stack([cref[chunk_rows(r0), :] for r0, cref in chains], axis=0)
        pre = _delta_prepare(load(qs), load(ks), load(vs), cc, half, consts)
        out = []
        for b in range(n_seq):
            s_f, s_b = states[2 * b], states[2 * b + 1]
            for g in range(group):
                o, s_f = _delta_apply(s_f, pre, b * group + g)
                o_f[chunk_rows(rows_f[b][g]), :] = o
                o, s_b = _delta_apply(s_b, pre, half + b * group + g)
                o_b[chunk_rows(rows_b[b][g]), :] = o
            out += [s_f, s_b]
        return tuple(out)

    init = tuple(s0_ref[b, d, 0] for b in range(n_seq) for d in range(2))
    final = trip(0, init) if n == group else lax.fori_loop(0, n // group, trip, init)
    for b in range(n_seq):
        st_ref[b, 0, 0] = final[2 * b]
        st_ref[b, 1, 0] = final[2 * b + 1]
    gate = gate_ref[...]
    o_ref[...] = (_rms(o_f[...] + o_b[...], ng_ref[...]) * (gate * jax.nn.sigmoid(gate))).astype(BF16)


DELTA_GROUP = 4
DELTA_BLOCK_ROWS = 1024


def _delta(z, batch, conv_w, alog_l, dtb_l, norm_g, s0):
    n = z.shape[0]
    t = n // batch
    h = DELTA_HEADS
    nb = max(1, DELTA_BLOCK_ROWS // t)
    rows = nb * t
    group = min(DELTA_GROUP, t // CHUNK)
    blk = lambda off: pl.BlockSpec((rows, LANES), lambda b, j: (b, off + j))
    cwb = lambda off: pl.BlockSpec((4, LANES), lambda b, j: (0, off + j))
    lg_blk = 22
    vec = pl.BlockSpec((1, LANES), lambda b, j: (0, 0))
    st_spec = pl.BlockSpec((nb, 2, 1, DELTA_DK, DELTA_DK), lambda b, j: (b, 0, j, 0, 0))
    return pl.pallas_call(
        functools.partial(_delta_body, seq=t, group=group),
        grid=(batch // nb, h),
        in_specs=[blk(0), blk(h), blk(2 * h), blk(3 * h), pl.BlockSpec((rows, LANES), lambda b, j: (b, lg_blk)),
                  cwb(0), cwb(h), cwb(2 * h), vec, vec, vec, st_spec],
        out_specs=[pl.BlockSpec((rows, LANES), lambda b, j: (b, j)), st_spec],
        out_shape=[jax.ShapeDtypeStruct((n, h * LANES), BF16),
                   jax.ShapeDtypeStruct((batch, 2, h, DELTA_DK, DELTA_DK), F32)],
        scratch_shapes=[pltpu.VMEM((rows, LANES), F32)] * 7,
        compiler_params=_cparams("parallel", "parallel"),
        name="delta",
    )(z, z, z, z, z, conv_w, conv_w, conv_w, alog_l, dtb_l, norm_g, s0)


def _outproj_body(x_ref, ya_ref, yb_ref, wa_ref, wb_ref, g1_ref, n2_ref, sc_ref, sh_ref, rwt_ref,
                  xo_ref, h2_ref, afft_ref):
    y = _mm(ya_ref[...], wa_ref[...]) + _mm(yb_ref[...], wb_ref[...])
    x = x_ref[...] + g1_ref[0] * y
    xo_ref[...] = x
    h = _rms(x, n2_ref[...]) * (1.0 + sc_ref[0]) + sh_ref[0]
    h2_ref[...] = h.astype(BF16)
    lt = _dot3(rwt_ref[...], h, _NT)
    et = jnp.exp(lt - jnp.max(lt, axis=0, keepdims=True))
    afft_ref[...] = et / jnp.sum(et, axis=0, keepdims=True)


def _outproj(x, ya, yb, wa, wb, gate1, n2, scale2, shift2, rwt, rows_per_mod):
    n, d = x.shape
    tm = 256
    per = rows_per_mod // tm
    half = ya.shape[1]
    row = lambda w: pl.BlockSpec((tm, w), lambda i: (i, 0))
    mod = pl.BlockSpec((1, 1, d), lambda i: (i // per, 0, 0))
    full = lambda a: pl.BlockSpec(a.shape, lambda i: (0,) * a.ndim)
    return pl.pallas_call(
        _outproj_body,
        grid=(n // tm,),
        in_specs=[row(d), row(half), row(half), full(wa), full(wb), mod, full(n2), mod, mod, full(rwt)],
        out_specs=[row(d), row(d), pl.BlockSpec((N_EXPERTS, tm), lambda i: (0, i))],
        out_shape=[jax.ShapeDtypeStruct((n, d), F32), jax.ShapeDtypeStruct((n, d), BF16),
                   jax.ShapeDtypeStruct((N_EXPERTS, n), F32)],
        compiler_params=_cparams("parallel"),
        name="outproj",
    )(x, ya, yb, wa, wb, gate1, n2, scale2, shift2, rwt)


def _route_body(aff_ref, pos_ref, off_ref, boff, *, cap):
    nb = aff_ref.shape[1]
    aff = aff_ref[...]

    def count(mask):
        c = jnp.sum(jnp.where(mask, 1.0, 0.0), axis=2, keepdims=True)
        return jnp.sum(c, axis=1, keepdims=True)

    def as_float(bits):
        return lax.bitcast_convert_type(bits, F32)

    top_bit = 29

    def search(i, thr):
        cand = thr | jnp.left_shift(jnp.int32(1), top_bit - i)
        return jnp.where(count(aff >= as_float(cand)) >= cap, cand, thr)

    thr = lax.fori_loop(0, top_bit + 1, search, jnp.zeros((N_EXPERTS, 1, 1), jnp.int32))
    above = aff >= as_float(thr + 1)
    tied = (aff >= as_float(thr)) & jnp.logical_not(above)
    need = cap - count(above)

    r = lax.broadcasted_iota(jnp.int32, (LANES, LANES), 0)
    c = lax.broadcasted_iota(jnp.int32, (LANES, LANES), 1)
    before = jnp.where(r < c, 1.0, 0.0).astype(BF16)
    ones = jnp.ones((LANES, LANES), BF16)
    rb = lax.broadcasted_iota(jnp.int32, (nb, nb), 0)
    cb = lax.broadcasted_iota(jnp.int32, (nb, nb), 1)
    blocks_before = jnp.where(cb < rb, 1.0, 0.0).astype(BF16)

    def excl_cumsum(mask):
        m2 = jnp.where(mask, 1.0, 0.0).astype(BF16).reshape(N_EXPERTS * nb, LANES)
        within = _mm(m2, before).reshape(N_EXPERTS, nb, LANES)
        tot = _mm(m2, ones).astype(BF16).reshape(N_EXPERTS, nb, LANES)
        for e in range(N_EXPERTS):
            boff[e] = _mm(blocks_before, tot[e])
        return within + boff[...]

    sel = above | (tied & (excl_cumsum(tied) < need))
    pos = excl_cumsum(sel)
    pos_ref[...] = jnp.where(sel, pos, -1.0e6).astype(jnp.int32)
    off_ref[...] = boff[...]


def _route(afft, cap):
    e, n = afft.shape
    nb = n // LANES
    shape = (e, nb, LANES)
    return pl.pallas_call(
        functools.partial(_route_body, cap=cap),
        out_shape=[jax.ShapeDtypeStruct(shape, jnp.int32), jax.ShapeDtypeStruct(shape, F32)],
        scratch_shapes=[pltpu.VMEM(shape, F32)],
        compiler_params=pltpu.CompilerParams(vmem_limit_bytes=VMEM_LIMIT),
        name="route",
    )(afft.reshape(shape))


def _tile_counts_small(off_ref, experts, i):
    small = None
    for e in experts:
        ok = off_ref[e, i + 1] - off_ref[e, i] <= SMALL_COUNT
        small = ok if small is None else jnp.logical_and(small, ok)
    return small


def _dispatch_body(off_ref, pos_ref, h_ref, xe_ref, acc, *, cap):
    g0 = pl.program_id(0) * DISPATCH_EXPERTS
    nb = pos_ref.shape[1]
    d = h_ref.shape[1]
    acc[:, 0:SUBLANES, :] = jnp.zeros((DISPATCH_EXPERTS, SUBLANES, d), F32)
    head_row = lax.broadcasted_iota(jnp.int32, (SUBLANES, d), 0)

    def scatter(i, win):
        xt = h_ref[pl.ds(pl.multiple_of(i * TOK_TILE, TOK_TILE), TOK_TILE), :]
        r = lax.broadcasted_iota(jnp.int32, (win, TOK_TILE), 0)
        starts = [off_ref[g0 + j, i] for j in range(DISPATCH_EXPERTS)]
        bases = [pl.multiple_of((s // SUBLANES) * SUBLANES, SUBLANES) for s in starts]
        onehots = [jnp.where(pos_ref[j, pl.ds(i, 1), :] - bases[j] == r, 1.0, 0.0) for j in range(DISPATCH_EXPERTS)]
        stacked = _mm(jnp.concatenate(onehots, axis=0).astype(BF16), xt)
        for j in range(DISPATCH_EXPERTS):
            start, base = starts[j], bases[j]
            rows = stacked[j * win:(j + 1) * win]
            head = pl.ds(base, SUBLANES)
            acc[j, head, :] = jnp.where(head_row >= start - base, rows[:SUBLANES], acc[j, head, :])
            acc[j, pl.ds(base + SUBLANES, win - SUBLANES), :] = rows[SUBLANES:]

    def body(i, carry):
        small = _tile_counts_small(off_ref, [g0 + j for j in range(DISPATCH_EXPERTS)], i)

        @pl.when(small)
        def _():
            scatter(i, SMALL_WIN)

        @pl.when(jnp.logical_not(small))
        def _():
            scatter(i, BIG_WIN)

        return carry

    lax.fori_loop(0, nb, body, 0)
    for j in range(DISPATCH_EXPERTS):
        xe_ref[j] = acc[j, 0:cap, :].astype(BF16)


def _dispatch(off, pos, h2, cap):
    n, d = h2.shape
    nb = n // TOK_TILE
    return pl.pallas_call(
        functools.partial(_dispatch_body, cap=cap),
        grid_spec=pltpu.PrefetchScalarGridSpec(
            num_scalar_prefetch=1,
            grid=(N_EXPERTS // DISPATCH_EXPERTS,),
            in_specs=[
                pl.BlockSpec((DISPATCH_EXPERTS, nb, LANES), lambda e, off: (e, 0, 0)),
                pl.BlockSpec((n, d), lambda e, off: (0, 0), pipeline_mode=pl.Buffered(1)),
            ],
            out_specs=pl.BlockSpec((DISPATCH_EXPERTS, cap, d), lambda e, off: (e, 0, 0)),
            scratch_shapes=[pltpu.VMEM((DISPATCH_EXPERTS, cap + BIG_WIN, d), F32)],
        ),
        out_shape=jax.ShapeDtypeStruct((N_EXPERTS, cap, d), BF16),
        compiler_params=_cparams("arbitrary"),
        name="dispatch",
    )(off, pos, h2)


def _ffn_body(xe_ref, w1_ref, w3_ref, w2_ref, ye_ref):
    xe = xe_ref[0]
    a = _mm(xe, w1_ref[0].astype(BF16))
    b = _mm(xe, w3_ref[0].astype(BF16))
    hid = (a * jax.nn.sigmoid(a) * b).astype(BF16)
    ye_ref[0] = _mm(hid, w2_ref[0].astype(BF16)).astype(BF16)


def _ffn(xe, w1, w3, w2):
    e, cap, d = xe.shape
    f = w1.shape[2]
    return pl.pallas_call(
        _ffn_body,
        grid=(e,),
        in_specs=[
            pl.BlockSpec((1, cap, d), lambda i: (i, 0, 0)),
            pl.BlockSpec((1, d, f), lambda i: (i, 0, 0)),
            pl.BlockSpec((1, d, f), lambda i: (i, 0, 0)),
            pl.BlockSpec((1, f, d), lambda i: (i, 0, 0)),
        ],
        out_specs=pl.BlockSpec((1, cap, d), lambda i: (i, 0, 0)),
        out_shape=jax.ShapeDtypeStruct((e, cap, d), BF16),
        compiler_params=_cparams("parallel"),
        name="ffn",
    )(xe, w1, w3, w2)


def _combine_body(off_ref, pos_ref, x_ref, afft_ref, g2_ref, ye_ref, o_ref, *, cap):
    i = pl.program_id(0)

    def gather(win):
        r = lax.broadcasted_iota(jnp.int32, (win, TOK_TILE), 0)
        hi, lo, wins = [], [], []
        for e in range(N_EXPERTS):
            start = jnp.minimum((off_ref[e, i] // SUBLANES) * SUBLANES, cap - win)
            start = pl.multiple_of(start, SUBLANES)
            gated = jnp.where(pos_ref[e] - start == r, afft_ref[e:e + 1, :], 0.0)
            g_hi, g_lo = _split(gated)
            hi.append(g_hi)
            lo.append(g_lo)
            wins.append(ye_ref[e, pl.ds(start, win), :])
        gates = jnp.concatenate([jnp.concatenate(hi, axis=0), jnp.concatenate(lo, axis=0)], axis=1)
        both = _mm(gates, jnp.concatenate(wins, axis=0), _TN)
        o_ref[...] = x_ref[...] + g2_ref[0] * (both[:TOK_TILE] + both[TOK_TILE:])

    small = None
    for e in range(N_EXPERTS):
        ok = off_ref[e, i + 1] - off_ref[e, i] <= COMBINE_SMALL_WIN - SUBLANES + 1
        small = ok if small is None else jnp.logical_and(small, ok)

    @pl.when(small)
    def _():
        gather(COMBINE_SMALL_WIN)

    @pl.when(jnp.logical_not(small))
    def _():
        gather(BIG_WIN)


def _combine(off, pos, x, afft, gate2, ye, rows_per_mod):
    n, d = x.shape
    nb = n // TOK_TILE
    per = rows_per_mod // TOK_TILE
    cap = ye.shape[1]
    return pl.pallas_call(
        functools.partial(_combine_body, cap=cap),
        grid_spec=pltpu.PrefetchScalarGridSpec(
            num_scalar_prefetch=1,
            grid=(nb,),
            in_specs=[
                pl.BlockSpec((N_EXPERTS, None, 1, LANES), lambda i, off: (0, i, 0, 0)),
                pl.BlockSpec((TOK_TILE, d), lambda i, off: (i, 0)),
                pl.BlockSpec((N_EXPERTS, TOK_TILE), lambda i, off: (0, i)),
                pl.BlockSpec((1, 1, d), lambda i, off: (i // per, 0, 0)),
                pl.BlockSpec(ye.shape, lambda i, off: (0, 0, 0), pipeline_mode=pl.Buffered(1)),
            ],
            out_specs=pl.BlockSpec((TOK_TILE, d), lambda i, off: (i, 0)),
        ),
        out_shape=jax.ShapeDtypeStruct((n, d), F32),
        compiler_params=_cparams("arbitrary"),
        name="combine",
    )(off, pos.reshape(N_EXPERTS, nb, 1, LANES), x, afft, gate2, ye)


def _block_diag(w):
    z = jnp.zeros_like(w[:, 0::2])
    top = jnp.concatenate([w[:, 0::2], z], axis=-1)
    bot = jnp.concatenate([z, w[:, 1::2]], axis=-1)
    return jnp.concatenate([top, bot], axis=-2)


def _rope_tables(n_tokens):
    rows = n_tokens // GRID_W
    row = jnp.repeat(jnp.arange(rows, dtype=F32), GRID_W)
    col = jnp.tile(jnp.arange(GRID_W, dtype=F32), rows)
    quarter = HEAD_DIM // 4
    inv_freq = ROPE_THETA ** (-jnp.arange(quarter, dtype=F32) / quarter)
    ang = jnp.concatenate([row[:, None] * inv_freq, col[:, None] * inv_freq], axis=-1)
    cos, sin = jnp.cos(ang), jnp.sin(ang)
    cos_l = jnp.tile(cos, (1, LANES // (HEAD_DIM // 2)))
    sin_l = jnp.tile(jnp.concatenate([-sin, sin], axis=-1), (1, LANES // HEAD_DIM))
    return cos_l, sin_l


def _moe(x_mid, h2, afft, gate2, w1, w3, w2, rows_per_mod):
    n = x_mid.shape[0]
    cap = max(1, EC_CAPACITY_FACTOR * n // N_EXPERTS)
    pos, off = _route(afft, cap)
    off_i = jnp.concatenate([off[:, :, 0].astype(jnp.int32), jnp.full((N_EXPERTS, 1), cap, jnp.int32)], axis=1)
    xe = _dispatch(off_i, pos, h2, cap)
    ye = _ffn(xe, w1, w3, w2)
    return _combine(off_i, pos, x_mid, afft, gate2, ye, rows_per_mod)


def kernel(x_prompt, x_sample, state_lru, cache_diff_k, cache_diff_v, state_delta, cache_gqa_k, cache_gqa_v,
           c, c_ctx, norm1_g, norm2_g, w_mod, b_mod, w_in_ab, lru_conv_w, lru_conv_b, lru_wa, lru_ba,
           lru_wi, lru_bi, lru_lam, diff_q_gain, diff_k_gain, diff_lam, diff_sub_gain, w_in_cd,
           delta_conv_w, delta_a_log, delta_dt_bias, delta_norm_g, gqa_q_gain, gqa_k_gain, w_out,
           router_w, exp_w1, exp_w3, exp_w2):
    d = D_MODEL
    dec_batch, dec_seq = x_sample.shape[:2]
    ctx_batch, ctx_seq = x_prompt.shape[:2]
    n_ctx_groups = c.shape[0]

    cs = jnp.concatenate([c_ctx[None, :], c, jnp.zeros((SUBLANES - 1 - n_ctx_groups, d), F32)], axis=0)
    mod = _mod_call(cs, w_mod, b_mod)

    r = jnp.arange(LANES)
    group_mean = jnp.where((r[:, None] // HEAD_DIM) == (r[None, :] // HEAD_DIM), 1.0 / HEAD_DIM, 0.0).astype(BF16)
    rope = _rope_tables(dec_seq)
    lane_pad = lambda v, at: jnp.zeros((1, LANES), F32).at[0, at:at + v.size].set(v.reshape(-1))

    def run_group(x3, mod_rows, ctx):
        batch, seq = x3.shape[:2]
        n = batch * seq
        x = x3.reshape(n, d)
        is_ctx = ctx is None
        news = []
        for l in range(DEPTH):
            li = l // 2
            m6 = mod[l, mod_rows[0]:mod_rows[1]].reshape(-1, 6, 1, d)
            shift1, scale1, gate1, shift2, scale2, gate2 = (m6[:, k] for k in range(6))
            g1 = norm1_g[l].reshape(1, d)
            if l % 2 == 0:
                z = _inproj(x, g1, scale1, shift1, w_in_ab[li].astype(BF16), seq if not is_ctx else n)
                h0 = jnp.zeros((batch, 2, LRU_WIDTH), F32) if is_ctx else state_lru[:, li]
                ya, lru_state = _lru(z, batch, lru_conv_w[li], lru_conv_b[li].reshape(1, -1),
                                     _block_diag(lru_wa[li]), lru_ba[li], _block_diag(lru_wi[li]), lru_bi[li],
                                     lru_lam[li], h0)
                gq = jnp.tile(diff_q_gain[li], 512 // HEAD_DIM).reshape(1, -1)
                gk = jnp.tile(diff_k_gain[li], 512 // HEAD_DIM).reshape(1, -1)
                v = z[:, 2048:2560]
                if is_ctx:
                    qn, kc, kb = _prep(z, 2, 3, 512, gq, gk, group_mean, None, seq)
                    kk = kb.reshape(batch, seq, 512)
                    vv = v.astype(BF16).reshape(batch, seq, 512)
                    news.append((lru_state, kc.reshape(batch, seq, DIFF_HEADS, 2, HEAD_DIM),
                                 v.reshape(batch, seq, DIFF_HEADS, 2 * HEAD_DIM)))
                else:
                    qn, kb = _prep(z, 2, 3, 512, gq, gk, group_mean, rope, seq)
                    past = cache_diff_k.shape[2]
                    kk = jnp.concatenate([cache_diff_k[:, li].reshape(batch, past, 512).astype(BF16),
                                          kb.reshape(batch, seq, 512)], axis=1)
                    vv = jnp.concatenate([cache_diff_v[:, li].reshape(batch, past, 512).astype(BF16),
                                          v.astype(BF16).reshape(batch, seq, 512)], axis=1)
                lam_init = 0.8 - 0.6 * math.exp(-0.3 * l)
                yb = _attn(qn, kk, vv, diff_lam[li], diff_sub_gain[li].reshape(1, -1), batch, True, lam_init)
            else:
                w = w_in_cd[li]
                w_p = jnp.concatenate([w[:, :2048], w[:, 2064:2832], w[:, 2048:2064],
                                       jnp.zeros((d, IN_CD_PAD - 2832), F32)], axis=1).astype(BF16)
                z = _inproj(x, g1, scale1, shift1, w_p, seq if not is_ctx else n)
                s0 = (jnp.zeros((batch, 2, DELTA_HEADS, DELTA_DK, DELTA_DK), F32) if is_ctx
                      else state_delta[:, li])
                ya, delta_state = _delta(z, batch, delta_conv_w[li], lane_pad(delta_a_log[li], 8),
                                         lane_pad(delta_dt_bias[li], 8), delta_norm_g[li].reshape(1, -1), s0)
                gq = jnp.tile(gqa_q_gain[li], 512 // HEAD_DIM).reshape(1, -1)
                gk = jnp.tile(gqa_k_gain[li], LANES // HEAD_DIM).reshape(1, -1)
                v = z[:, 2688:2816]
                dup = lambda a: jnp.repeat(a.reshape(a.shape[0], a.shape[1], GQA_KV_HEADS, 1, HEAD_DIM), 2,
                                           axis=3).reshape(a.shape[0], a.shape[1], 2 * LANES)
                if is_ctx:
                    qn, kc, kb = _prep(z, 4, 20, LANES, gq, gk, group_mean, None, seq)
                    kk = dup(kb.reshape(batch, seq, LANES))
                    vv = dup(v.astype(BF16).reshape(batch, seq, LANES))
                    news.append((delta_state, kc.reshape(batch, seq, GQA_KV_HEADS, HEAD_DIM),
                                 v.reshape(batch, seq, GQA_KV_HEADS, HEAD_DIM)))
                else:
                    qn, kb = _prep(z, 4, 20, LANES, gq, gk, group_mean, rope, seq)
                    past = cache_gqa_k.shape[2]
                    kk = dup(jnp.concatenate([cache_gqa_k[:, li].reshape(batch, past, LANES).astype(BF16),
                                              kb.reshape(batch, seq, LANES)], axis=1))
                    vv = dup(jnp.concatenate([cache_gqa_v[:, li].reshape(batch, past, LANES).astype(BF16),
                                              v.astype(BF16).reshape(batch, seq, LANES)], axis=1))
                yb = _attn(qn, kk, vv, jnp.zeros((4, HEAD_DIM), F32), jnp.zeros((1, LANES), F32), batch, False, 0.0)
            wo = w_out[l].astype(BF16)
            x_mid, h2, afft = _outproj(x, ya, yb, wo[:512], wo[512:], gate1, norm2_g[l].reshape(1, d),
                                       scale2, shift2, router_w[l].T, seq if not is_ctx else n)
            x = _moe(x_mid, h2, afft, gate2, exp_w1[l], exp_w3[l], exp_w2[l], seq if not is_ctx else n)
        return x.reshape(batch, seq, d), news

    y_prompt, news = run_group(x_prompt, (0, 1), None)
    y_sample, _ = run_group(x_sample, (1, 1 + dec_batch), True)

    dtype = x_prompt.dtype
    even, odd = news[0::2], news[1::2]
    stack = lambda items, k: jnp.stack([it[k] for it in items], axis=1).astype(dtype)
    return (y_prompt, y_sample, stack(even, 0), stack(even, 1), stack(even, 2),
            stack(odd, 0), stack(odd, 1), stack(odd, 2))
```

```python
import functools
import math

import jax
import jax.numpy as jnp
from jax import lax
from jax.experimental import pallas as pl
from jax.experimental.pallas import tpu as pltpu

F32 = jnp.float32
BF16 = jnp.bfloat16

D_MODEL = 1024
DEPTH = 4
GRID_W = 64
HEAD_DIM = 64
ROPE_THETA = 10000.0
RMS_EPS = 1e-6
LRU_WIDTH = 512
LRU_BLOCKS = 8
LRU_C = 8.0
DIFF_HEADS = 4
DELTA_HEADS = 4
DELTA_DK = 128
GQA_HEADS = 8
GQA_KV_HEADS = 2
N_EXPERTS = 16
EC_CAPACITY_FACTOR = 2
EXPERT_FF = 512
IN_AB = 2560
IN_CD_PAD = 2944

LANES = 128
SUBLANES = 8
VMEM_LIMIT = 56 * 1024 * 1024

Q_SCALE = HEAD_DIM ** -0.5 * math.log2(math.e)
CHUNK = 128
TOK_TILE = 128
DISPATCH_EXPERTS = 4
SMALL_WIN = 48
SMALL_COUNT = SMALL_WIN - 2 * SUBLANES + 1
BIG_WIN = TOK_TILE + 2 * SUBLANES
COMBINE_SMALL_WIN = SMALL_WIN


def _cparams(*sem):
    return pltpu.CompilerParams(dimension_semantics=sem, vmem_limit_bytes=VMEM_LIMIT)


def _mm(a, b, dims=(((1,), (0,)), ((), ()))):
    return lax.dot_general(a, b, dims, preferred_element_type=F32)


_NT = (((1,), (1,)), ((), ()))
_TN = (((0,), (0,)), ((), ()))


def _dot1(a, b, dims=(((1,), (0,)), ((), ()))):
    return _mm(a.astype(BF16), b.astype(BF16), dims)


def _split(x):
    hi = x.astype(BF16)
    lo = (x - hi.astype(F32)).astype(BF16)
    return hi, lo


def _dot3(a, b, dims=(((1,), (0,)), ((), ()))):
    ah, al = _split(a)
    bh, bl = _split(b)
    return _mm(ah, bh, dims) + (_mm(ah, bl, dims) + _mm(al, bh, dims))


def _dot_exact_rhs(m_bf16, x):
    x1 = x.astype(BF16)
    r1 = x - x1.astype(F32)
    x2 = r1.astype(BF16)
    x3 = (r1 - x2.astype(F32)).astype(BF16)
    return _mm(m_bf16, x1) + (_mm(m_bf16, x2) + _mm(m_bf16, x3))


def _dot_exact_lhs(x, m_bf16):
    x1 = x.astype(BF16)
    r1 = x - x1.astype(F32)
    x2 = r1.astype(BF16)
    x3 = (r1 - x2.astype(F32)).astype(BF16)
    return _mm(x1, m_bf16) + (_mm(x2, m_bf16) + _mm(x3, m_bf16))


def _rms(x, g):
    return x * lax.rsqrt(jnp.mean(x * x, axis=-1, keepdims=True) + RMS_EPS) * g


def _mod_body(c_ref, w_ref, b_ref, o_ref):
    c = c_ref[...]
    a = c * jax.nn.sigmoid(c)
    o_ref[0] = _dot3(a, w_ref[0]) + b_ref[0]


def _mod_call(cs, w_mod, b_mod):
    rows, d = cs.shape
    width = w_mod.shape[2]
    tn = 512
    return pl.pallas_call(
        _mod_body,
        grid=(DEPTH, width // tn),
        in_specs=[
            pl.BlockSpec((rows, d), lambda l, j: (0, 0)),
            pl.BlockSpec((1, d, tn), lambda l, j: (l, 0, j)),
            pl.BlockSpec((1, 1, tn), lambda l, j: (l, 0, j)),
        ],
        out_specs=pl.BlockSpec((1, rows, tn), lambda l, j: (l, 0, j)),
        out_shape=jax.ShapeDtypeStruct((DEPTH, rows, width), F32),
        compiler_params=_cparams("parallel", "parallel"),
        name="mod",
    )(cs, w_mod, b_mod.reshape(DEPTH, 1, width))


def _inproj_body(x_ref, g_ref, sc_ref, sh_ref, w_ref, z_ref):
    h = _rms(x_ref[...], g_ref[...]) * (1.0 + sc_ref[0]) + sh_ref[0]
    z_ref[...] = _mm(h.astype(BF16), w_ref[...])


def _inproj(x, g, scale, shift, w, rows_per_mod):
    n, d = x.shape
    width = w.shape[1]
    tm = 256
    per = rows_per_mod // tm
    return pl.pallas_call(
        _inproj_body,
        grid=(n // tm,),
        in_specs=[
            pl.BlockSpec((tm, d), lambda i: (i, 0)),
            pl.BlockSpec((1, d), lambda i: (0, 0)),
            pl.BlockSpec((1, 1, d), lambda i: (i // per, 0, 0)),
            pl.BlockSpec((1, 1, d), lambda i: (i // per, 0, 0)),
            pl.BlockSpec((d, width), lambda i: (0, 0)),
        ],
        out_specs=pl.BlockSpec((tm, width), lambda i: (i, 0)),
        out_shape=jax.ShapeDtypeStruct((n, width), F32),
        compiler_params=_cparams("parallel"),
        name="inproj",
    )(x, g, scale, shift, w)


def _shift_rows(x, k, t_iota, seq):
    rows = x.shape[0]
    if k > 0:
        return jnp.where(t_iota >= k, pltpu.roll(x, k, 0), 0.0)
    return jnp.where(t_iota < seq + k, pltpu.roll(x, rows + k, 0), 0.0)


def _conv4(x, w_ref, t_iota, seq):
    return (
        w_ref[0:1, :] * _shift_rows(x, 2, t_iota, seq)
        + w_ref[1:2, :] * _shift_rows(x, 1, t_iota, seq)
        + w_ref[2:3, :] * x
        + w_ref[3:4, :] * _shift_rows(x, -1, t_iota, seq)
    )


def _tile_scan(a, b, row, reverse):
    for d in (1, 2, 4):
        if reverse:
            ok = row < SUBLANES - d
            a_sh = pltpu.roll(a, SUBLANES - d, 0)
            b_sh = pltpu.roll(b, SUBLANES - d, 0)
        else:
            ok = row >= d
            a_sh = pltpu.roll(a, d, 0)
            b_sh = pltpu.roll(b, d, 0)
        a_sh = jnp.where(ok, a_sh, 1.0)
        b_sh = jnp.where(ok, b_sh, 0.0)
        b = a * b_sh + b
        a = a * a_sh
    return a, b


def _lru_body(xa_ref, ga_ref, cw_ref, cb_ref, wa_ref, ba_ref, wi_ref, bi_ref, lam_ref, h0_ref,
              y_ref, st_ref, af, bf, ab, bb, hf, hb, *, seq):
    rows = xa_ref.shape[0]
    n_seq = rows // seq
    t_iota = lax.broadcasted_iota(jnp.int32, (rows, LANES), 0) % seq
    u = _conv4(xa_ref[...], cw_ref, t_iota, seq) + cb_ref[...]
    ub = u.astype(BF16)
    for d, (a_s, b_s) in enumerate(((af, bf), (ab, bb))):
        r = jax.nn.sigmoid(_mm(ub, wa_ref[d].astype(BF16)) + ba_ref[d:d + 1, :])
        i = jax.nn.sigmoid(_mm(ub, wi_ref[d].astype(BF16)) + bi_ref[d:d + 1, :])
        log_a = -LRU_C * r * jax.nn.softplus(-lam_ref[d:d + 1, :])
        a = jnp.exp(log_a)
        a_s[...] = a
        b_s[...] = jnp.sqrt(1.0 - a * a) * (i * u)

    n8 = seq // SUBLANES
    row = lax.broadcasted_iota(jnp.int32, (SUBLANES, LANES), 0)

    def step(i, carry):
        out = []
        for b in range(n_seq):
            h_f, h_b = carry[2 * b], carry[2 * b + 1]
            r0 = pl.multiple_of(b * seq + i * SUBLANES, SUBLANES)
            a_c, b_c = _tile_scan(af[pl.ds(r0, SUBLANES), :], bf[pl.ds(r0, SUBLANES), :], row, False)
            hf_t = a_c * h_f + b_c
            hf[pl.ds(r0, SUBLANES), :] = hf_t
            r1 = pl.multiple_of(b * seq + (n8 - 1 - i) * SUBLANES, SUBLANES)
            a_c, b_c = _tile_scan(ab[pl.ds(r1, SUBLANES), :], bb[pl.ds(r1, SUBLANES), :], row, True)
            hb_t = a_c * h_b + b_c
            hb[pl.ds(r1, SUBLANES), :] = hb_t
            out += [hf_t[SUBLANES - 1:SUBLANES, :], hb_t[0:1, :]]
        return tuple(out)

    init = tuple(h0_ref[b, d:d + 1, :] for b in range(n_seq) for d in range(2))
    final = lax.fori_loop(0, n8, step, init, unroll=2)
    y_ref[...] = ((hf[...] + hb[...]) * jax.nn.gelu(ga_ref[...])).astype(BF16)
    for b in range(n_seq):
        st_ref[b, 0:1, :] = final[2 * b]
        st_ref[b, 1:2, :] = final[2 * b + 1]


LRU_BLOCK_ROWS = 1024


def _lru(z, batch, conv_w, conv_b, wa_bd, ba, wi_bd, bi, lam, h0):
    n = z.shape[0]
    seq = n // batch
    nb = max(1, LRU_BLOCK_ROWS // seq)
    t = nb * seq
    nj = LRU_WIDTH // LANES
    return pl.pallas_call(
        functools.partial(_lru_body, seq=seq),
        grid=(batch // nb, nj),
        in_specs=[
            pl.BlockSpec((t, LANES), lambda b, j: (b, j)),
            pl.BlockSpec((t, LANES), lambda b, j: (b, nj + j)),
            pl.BlockSpec((4, LANES), lambda b, j: (0, j)),
            pl.BlockSpec((1, LANES), lambda b, j: (0, j)),
            pl.BlockSpec((2, None, LANES, LANES), lambda b, j: (0, j, 0, 0)),
            pl.BlockSpec((2, LANES), lambda b, j: (0, j)),
            pl.BlockSpec((2, None, LANES, LANES), lambda b, j: (0, j, 0, 0)),
            pl.BlockSpec((2, LANES), lambda b, j: (0, j)),
            pl.BlockSpec((2, LANES), lambda b, j: (0, j)),
            pl.BlockSpec((nb, 2, LANES), lambda b, j: (b, 0, j)),
        ],
        out_specs=[
            pl.BlockSpec((t, LANES), lambda b, j: (b, j)),
            pl.BlockSpec((nb, 2, LANES), lambda b, j: (b, 0, j)),
        ],
        out_shape=[
            jax.ShapeDtypeStruct((n, LRU_WIDTH), BF16),
            jax.ShapeDtypeStruct((batch, 2, LRU_WIDTH), F32),
        ],
        scratch_shapes=[pltpu.VMEM((t, LANES), F32)] * 6,
        compiler_params=_cparams("parallel", "parallel"),
        name="lru",
    )(z, z, conv_w, conv_b, wa_bd, ba, wi_bd, bi, lam, h0)


def _head_norm(x, g_ref, gm_ref):
    outs = []
    for j in range(x.shape[1] // LANES):
        xs = x[:, j * LANES:(j + 1) * LANES]
        sq = xs * xs
        hi, lo = _split(sq)
        ms = _mm(hi, gm_ref[...]) + _mm(lo, gm_ref[...])
        outs.append(xs * lax.rsqrt(ms + RMS_EPS) * g_ref[:, j * LANES:(j + 1) * LANES])
    return outs


def _rope(xs, cos, sin, first_half):
    swapped = jnp.where(first_half, pltpu.roll(xs, LANES - HEAD_DIM // 2, 1), pltpu.roll(xs, HEAD_DIM // 2, 1))
    return xs * cos + swapped * sin


def _prep_ctx_body(q_ref, k_ref, gq_ref, gk_ref, gm_ref, qo_ref, kc_ref, kb_ref):
    qs = _head_norm(q_ref[...], gq_ref, gm_ref)
    ks = _head_norm(k_ref[...], gk_ref, gm_ref)
    for j, xs in enumerate(qs):
        qo_ref[:, j * LANES:(j + 1) * LANES] = (xs * Q_SCALE).astype(BF16)
    for j, xs in enumerate(ks):
        kc_ref[:, j * LANES:(j + 1) * LANES] = xs
        kb_ref[:, j * LANES:(j + 1) * LANES] = xs.astype(BF16)


def _prep_rope_body(q_ref, k_ref, gq_ref, gk_ref, gm_ref, cos_ref, sin_ref, qo_ref, kb_ref):
    qs = _head_norm(q_ref[...], gq_ref, gm_ref)
    ks = _head_norm(k_ref[...], gk_ref, gm_ref)
    cos = cos_ref[...]
    sin = sin_ref[...]
    lane = lax.broadcasted_iota(jnp.int32, cos.shape, 1)
    first_half = (lane % HEAD_DIM) < HEAD_DIM // 2
    for j, xs in enumerate(qs):
        qo_ref[:, j * LANES:(j + 1) * LANES] = (_rope(xs, cos, sin, first_half) * Q_SCALE).astype(BF16)
    for j, xs in enumerate(ks):
        kb_ref[:, j * LANES:(j + 1) * LANES] = _rope(xs, cos, sin, first_half).astype(BF16)


def _prep(z, q_blk, k_blk, kw, gq, gk, gm, rope, seq):
    n = z.shape[0]
    tm = 256
    qw = 512
    in_specs = [
        pl.BlockSpec((tm, qw), lambda i: (i, q_blk)),
        pl.BlockSpec((tm, kw), lambda i: (i, k_blk)),
        pl.BlockSpec((1, qw), lambda i: (0, 0)),
        pl.BlockSpec((1, kw), lambda i: (0, 0)),
        pl.BlockSpec((LANES, LANES), lambda i: (0, 0)),
    ]
    q_out = (pl.BlockSpec((tm, qw), lambda i: (i, 0)), jax.ShapeDtypeStruct((n, qw), BF16))
    kb_out = (pl.BlockSpec((tm, kw), lambda i: (i, 0)), jax.ShapeDtypeStruct((n, kw), BF16))
    if rope is None:
        kc_out = (pl.BlockSpec((tm, kw), lambda i: (i, 0)), jax.ShapeDtypeStruct((n, kw), F32))
        outs = (q_out, kc_out, kb_out)
        body = _prep_ctx_body
        args = (z, z, gq, gk, gm)
    else:
        per = seq // tm
        in_specs += [pl.BlockSpec((tm, LANES), lambda i: (i % per, 0))] * 2
        outs = (q_out, kb_out)
        body = _prep_rope_body
        args = (z, z, gq, gk, gm, rope[0], rope[1])
    return pl.pallas_call(
        body,
        grid=(n // tm,),
        in_specs=in_specs,
        out_specs=[o[0] for o in outs],
        out_shape=[o[1] for o in outs],
        compiler_params=_cparams("parallel"),
        name="prep",
    )(*args)


def _attn_body(q_ref, *refs, n_parts, diff, lam_init, tk, share):
    k_refs, v_refs = refs[:n_parts], refs[n_parts:2 * n_parts]
    lam_ref, sg_ref, o_ref, s_scr = refs[2 * n_parts:]
    for j in range(q_ref.shape[1] // LANES):
        q_cols = slice(j * LANES, (j + 1) * LANES)
        kv_cols = slice((j // share) * LANES, (j // share + 1) * LANES)
        o_ref[:, q_cols] = _attn_pair(q_ref[:, q_cols], k_refs, v_refs, kv_cols, lam_ref, sg_ref, s_scr,
                                      diff=diff, lam_init=lam_init, tk=tk)


def _attn_pair(q, k_refs, v_refs, kv_cols, lam_ref, sg_ref, s_scr, *, diff, lam_init, tk):
    tq = q.shape[0]
    lane = lax.broadcasted_iota(jnp.int32, (tq, LANES), 1)
    zero = jnp.zeros_like(q)
    heads = (jnp.where(lane < HEAD_DIM, q, zero), jnp.where(lane >= HEAD_DIM, q, zero))
    blocks, col = [], 0
    for part, k_ref in enumerate(k_refs):
        for r0 in range(0, k_ref.shape[1], tk):
            blocks.append((part, r0, col))
            col += tk

    rms = [jnp.full((tq, LANES), -jnp.inf, F32)] * 2
    for part, r0, col in blocks:
        k = k_refs[part][0, r0:r0 + tk, kv_cols]
        for h, qh in enumerate(heads):
            s = _mm(qh, k, _NT)
            s_scr[h, :, col:col + tk] = s
            for c in range(tk // LANES):
                rms[h] = jnp.maximum(rms[h], s[:, c * LANES:(c + 1) * LANES])
    ms = [jnp.max(rm, axis=-1, keepdims=True) for rm in rms]

    ones = jnp.ones((tk, LANES), BF16)
    accs = [jnp.zeros((tq, 2 * LANES), F32)] * 2
    for part, r0, col in blocks:
        v1 = jnp.concatenate([v_refs[part][0, r0:r0 + tk, kv_cols], ones], axis=1)
        for h in range(2):
            accs[h] = accs[h] + _mm(jnp.exp2(s_scr[h, :, col:col + tk] - ms[h]).astype(BF16), v1)
    o0, o1 = (acc[:, :LANES] / acc[:, LANES:] for acc in accs)
    if diff:
        lv = lam_ref[...]
        lam = (jnp.exp(jnp.sum(lv[0:1] * lv[1:2], axis=-1, keepdims=True))
               - jnp.exp(jnp.sum(lv[2:3] * lv[3:4], axis=-1, keepdims=True)) + lam_init)
        y = _rms(o0 - lam * o1, sg_ref[...]) * (1.0 - lam_init)
    else:
        y = jnp.where(lane < HEAD_DIM, o0, o1)
    return y.astype(BF16)


ATTN_SHORT_KEYS = 1024


def _attn(qn, k_parts, v_parts, lam_p, sub_gain, batch, diff, lam_init):
    n, qw = qn.shape
    t = n // batch
    tq = 256
    nq = t // tq
    tkv = sum(k.shape[1] for k in k_parts)
    tk = min(512, min(k.shape[1] for k in k_parts))
    q_blocks = qw // LANES
    kv_blocks = k_parts[0].shape[2] // LANES
    share = q_blocks // kv_blocks
    per_step = q_blocks if tkv <= ATTN_SHORT_KEYS else 1
    kv_per_step = max(1, per_step // share)
    kv_map = (lambda b, j, i: (b, 0, j)) if per_step > 1 else (lambda b, j, i: (b, 0, j // share))
    kv_specs = [pl.BlockSpec((1, k.shape[1], kv_per_step * LANES), kv_map) for k in k_parts]
    return pl.pallas_call(
        functools.partial(_attn_body, n_parts=len(k_parts), diff=diff, lam_init=lam_init, tk=tk,
                          share=share if per_step > 1 else 1),
        grid=(batch, q_blocks // per_step, nq),
        in_specs=[pl.BlockSpec((tq, per_step * LANES), lambda b, j, i: (b * nq + i, j))] + kv_specs + kv_specs + [
            pl.BlockSpec((4, HEAD_DIM), lambda b, j, i: (0, 0)),
            pl.BlockSpec((1, LANES), lambda b, j, i: (0, 0)),
        ],
        out_specs=pl.BlockSpec((tq, per_step * LANES), lambda b, j, i: (b * nq + i, j)),
        out_shape=jax.ShapeDtypeStruct((n, qw), BF16),
        scratch_shapes=[pltpu.VMEM((2, tq, tkv), F32)],
        compiler_params=_cparams("parallel", "parallel", "parallel"),
        name="attn",
    )(qn, *k_parts, *v_parts, lam_p, sub_gain)


INV_BASE = 8


_BNN = (((2,), (1,)), ((0,), (0,)))
_BNT = (((2,), (2,)), ((0,), (0,)))


def _bdot3(a, b):
    ah, al = _split(a)
    bh, bl = _split(b)
    return _mm(ah, bh, _BNN) + (_mm(ah, bl, _BNN) + _mm(al, bh, _BNN))


def _unit_tri_inverse(a, eye, blocks):
    same_base, level_masks = blocks
    d = jnp.where(same_base, a, 0.0)
    p = eye - d
    q = _bdot3(d, d)
    p = p + _bdot3(p, q)
    q = _bdot3(q, q)
    x = p + _bdot3(p, q)
    for m in level_masks:
        x = x - _bdot3(_bdot3(x, jnp.where(m, a, 0.0)), x)
    return x


def _delta_prepare(qc, kc, vc, cc, n_fwd, consts):
    incl, strict, cum_m, eye, blocks = consts
    x1 = cc.astype(BF16)
    r1 = cc - x1.astype(F32)
    x2 = r1.astype(BF16)
    x3 = (r1 - x2.astype(F32)).astype(BF16)
    cum = _mm(cum_m, x1, _BNN) + (_mm(cum_m, x2, _BNN) + _mm(cum_m, x3, _BNN))
    cum_t = jnp.swapaxes(cum, 1, 2)
    g_col = cum[:, :, 2:3]
    g_row = cum_t[:, 2:3, :]
    beta = cc[:, :, 0:1]
    g_last = jnp.concatenate([g_row[:n_fwd, :, CHUNK - 1:CHUNK], g_row[n_fwd:, :, 0:1]], axis=0)
    decay = jnp.exp(jnp.where(incl > 0.0, g_col - g_row, -jnp.inf))
    kb = kc * beta
    kcb = kc.astype(BF16)
    a = jnp.where(strict > 0.0, _mm(kb.astype(BF16), kcb, _BNT) * decay, 0.0)
    inv = _unit_tri_inverse(a, eye, blocks)
    e_g = jnp.exp(g_col)
    sol = _bdot3(inv, jnp.concatenate([kb * e_g, vc * beta], axis=2))
    w = sol[:, :, :DELTA_DK]
    u = sol[:, :, DELTA_DK:]
    qk = jnp.where(incl > 0.0, _mm(qc.astype(BF16), kcb, _BNT) * decay, 0.0)
    k_tail = kc * jnp.exp(g_last - g_col)
    return (w.astype(BF16), u, qk.astype(BF16), (qc * e_g).astype(BF16), k_tail.astype(BF16), jnp.exp(g_last))


def _delta_apply(s, pre, g):
    w, u, qk, q_dec, k_tail, chunk_decay = (t[g] for t in pre)
    sb = s.astype(BF16)
    v_new = u - _mm(w, sb)
    vb = v_new.astype(BF16)
    o = _mm(q_dec, sb) + _mm(qk, vb)
    return o, s * chunk_decay + _mm(k_tail, vb, _TN)


def _delta_body(q_ref, k_ref, v_ref, gate_ref, lg_ref, cwq_ref, cwk_ref, cwv_ref, alog_ref, dtb_ref, ng_ref,
                s0_ref, o_ref, st_ref, qs, ks, vs, cs_f, cs_b, o_f, o_b, *, seq, group):
    rows = q_ref.shape[0]
    head = pl.program_id(1)
    t_iota = lax.broadcasted_iota(jnp.int32, (rows, LANES), 0) % seq
    lane = lax.broadcasted_iota(jnp.int32, (rows, LANES), 1)

    def conv_silu(ref, cw):
        x = _conv4(ref[...], cw, t_iota, seq)
        return x * jax.nn.sigmoid(x)

    q = conv_silu(q_ref, cwq_ref)
    k = conv_silu(k_ref, cwk_ref)
    qs[...] = q * lax.rsqrt(jnp.sum(q * q, axis=-1, keepdims=True) + 1e-6) * DELTA_DK ** -0.5
    ks[...] = k * lax.rsqrt(jnp.sum(k * k, axis=-1, keepdims=True) + 1e-6)
    vs[...] = conv_silu(v_ref, cwv_ref)
    lg = lg_ref[...]
    beta = jax.nn.sigmoid(lg)
    g_all = -jnp.exp(alog_ref[...]) * jax.nn.softplus(lg + dtb_ref[...])

    def col(arr, idx):
        return jnp.sum(jnp.where(lane == idx, arr, 0.0), axis=1, keepdims=True)

    cs_f[...] = jnp.where(lane == 0, col(beta, head), jnp.where(lane == 2, col(g_all, 2 * DELTA_HEADS + head), 0.0))
    cs_b[...] = jnp.where(lane == 0, col(beta, DELTA_HEADS + head),
                          jnp.where(lane == 2, col(g_all, 3 * DELTA_HEADS + head), 0.0))

    n_seq = rows // seq
    half = n_seq * group
    r = lax.broadcasted_iota(jnp.int32, (CHUNK, CHUNK), 0)
    c = lax.broadcasted_iota(jnp.int32, (CHUNK, CHUNK), 1)
    eye = jnp.where(r == c, 1.0, 0.0)
    level_masks = []
    size = INV_BASE
    while size < CHUNK:
        level_masks.append(((r // (2 * size)) == (c // (2 * size))) & ((r // size) != (c // size)))
        size *= 2
    blocks = ((r // INV_BASE) == (c // INV_BASE), level_masks)

    def per_direction(fwd, bwd):
        stack = lambda m: jnp.broadcast_to(jnp.where(m, 1.0, 0.0)[None], (half, CHUNK, CHUNK))
        return jnp.concatenate([stack(fwd), stack(bwd)], axis=0)

    incl = per_direction(r >= c, r <= c)
    consts = (incl, per_direction(r > c, r < c), incl.astype(BF16), eye, blocks)
    n = seq // CHUNK

    def chunk_rows(r0):
        return pl.ds(r0 if isinstance(r0, int) else pl.multiple_of(r0, CHUNK), CHUNK)

    def trip(i, states):
        rows_f = [[b * seq + (i * group + g) * CHUNK for g in range(group)] for b in range(n_seq)]
        rows_b = [[b * seq + (n - 1 - (i * group + g)) * CHUNK for g in range(group)] for b in range(n_seq)]
        chains = [(r0, cs_f) for rb in rows_f for r0 in rb] + [(r0, cs_b) for rb in rows_b for r0 in rb]
        load = lambda ref: jnp.stack([ref[chunk_rows(r0), :] for r0, _ in chains], axis=0)
        cc = jnp.stack([cref[chunk_rows(r0), :] for r0, cref in chains], axis=0)
        pre = _delta_prepare(load(qs), load(ks), load(vs), cc, half, consts)
        out = []
        for b in range(n_seq):
            s_f, s_b = states[2 * b], states[2 * b + 1]
            for g in range(group):
                o, s_f = _delta_apply(s_f, pre, b * group + g)
                o_f[chunk_rows(rows_f[b][g]), :] = o
                o, s_b = _delta_apply(s_b, pre, half + b * group + g)
                o_b[chunk_rows(rows_b[b][g]), :] = o
            out += [s_f, s_b]
        return tuple(out)

    init = tuple(s0_ref[b, d, 0] for b in range(n_seq) for d in range(2))
    final = trip(0, init) if n == group else lax.fori_loop(0, n // group, trip, init)
    for b in range(n_seq):
        st_ref[b, 0, 0] = final[2 * b]
        st_ref[b, 1, 0] = final[2 * b + 1]
    gate = gate_ref[...]
    o_ref[...] = (_rms(o_f[...] + o_b[...], ng_ref[...]) * (gate * jax.nn.sigmoid(gate))).astype(BF16)


DELTA_GROUP = 4
DELTA_BLOCK_ROWS = 1024


def _delta(z, batch, conv_w, alog_l, dtb_l, norm_g, s0):
    n = z.shape[0]
    t = n // batch
    h = DELTA_HEADS
    nb = max(1, DELTA_BLOCK_ROWS // t)
    rows = nb * t
    group = min(DELTA_GROUP, t // CHUNK)
    blk = lambda off: pl.BlockSpec((rows, LANES), lambda b, j: (b, off + j))
    cwb = lambda off: pl.BlockSpec((4, LANES), lambda b, j: (0, off + j))
    lg_blk = 22
    vec = pl.BlockSpec((1, LANES), lambda b, j: (0, 0))
    st_spec = pl.BlockSpec((nb, 2, 1, DELTA_DK, DELTA_DK), lambda b, j: (b, 0, j, 0, 0))
    return pl.pallas_call(
        functools.partial(_delta_body, seq=t, group=group),
        grid=(batch // nb, h),
        in_specs=[blk(0), blk(h), blk(2 * h), blk(3 * h), pl.BlockSpec((rows, LANES), lambda b, j: (b, lg_blk)),
                  cwb(0), cwb(h), cwb(2 * h), vec, vec, vec, st_spec],
        out_specs=[pl.BlockSpec((rows, LANES), lambda b, j: (b, j)), st_spec],
        out_shape=[jax.ShapeDtypeStruct((n, h * LANES), BF16),
                   jax.ShapeDtypeStruct((batch, 2, h, DELTA_DK, DELTA_DK), F32)],
        scratch_shapes=[pltpu.VMEM((rows, LANES), F32)] * 7,
        compiler_params=_cparams("parallel", "parallel"),
        name="delta",
    )(z, z, z, z, z, conv_w, conv_w, conv_w, alog_l, dtb_l, norm_g, s0)


def _outproj_body(x_ref, ya_ref, yb_ref, wa_ref, wb_ref, g1_ref, n2_ref, sc_ref, sh_ref, rwt_ref,
                  xo_ref, h2_ref, afft_ref):
    for c in range(x_ref.shape[0] // OUTPROJ_ROWS):
        rows = slice(c * OUTPROJ_ROWS, (c + 1) * OUTPROJ_ROWS)
        y = _mm(ya_ref[rows, :], wa_ref[...]) + _mm(yb_ref[rows, :], wb_ref[...])
        x = x_ref[rows, :] + g1_ref[0] * y
        xo_ref[rows, :] = x
        h = _rms(x, n2_ref[...]) * (1.0 + sc_ref[0]) + sh_ref[0]
        h2_ref[rows, :] = h.astype(BF16)
        lt = _dot3(rwt_ref[...], h, _NT)
        et = jnp.exp(lt - jnp.max(lt, axis=0, keepdims=True))
        afft_ref[:, rows] = et / jnp.sum(et, axis=0, keepdims=True)


OUTPROJ_ROWS = 256


def _outproj(x, ya, yb, wa, wb, gate1, n2, scale2, shift2, rwt, rows_per_mod):
    n, d = x.shape
    tm = 2 * OUTPROJ_ROWS
    per = rows_per_mod // tm
    half = ya.shape[1]
    row = lambda w: pl.BlockSpec((tm, w), lambda i: (i, 0))
    mod = pl.BlockSpec((1, 1, d), lambda i: (i // per, 0, 0))
    full = lambda a: pl.BlockSpec(a.shape, lambda i: (0,) * a.ndim)
    return pl.pallas_call(
        _outproj_body,
        grid=(n // tm,),
        in_specs=[row(d), row(half), row(half), full(wa), full(wb), mod, full(n2), mod, mod, full(rwt)],
        out_specs=[row(d), row(d), pl.BlockSpec((N_EXPERTS, tm), lambda i: (0, i))],
        out_shape=[jax.ShapeDtypeStruct((n, d), F32), jax.ShapeDtypeStruct((n, d), BF16),
                   jax.ShapeDtypeStruct((N_EXPERTS, n), F32)],
        compiler_params=_cparams("parallel"),
        name="outproj",
    )(x, ya, yb, wa, wb, gate1, n2, scale2, shift2, rwt)


def _route_body(aff_ref, pos_ref, off_ref, boff, *, cap):
    nb = aff_ref.shape[1]
    aff = aff_ref[...]

    def count(mask):
        c = jnp.sum(jnp.where(mask, 1.0, 0.0), axis=2, keepdims=True)
        return jnp.sum(c, axis=1, keepdims=True)

    def as_float(bits):
        return lax.bitcast_convert_type(bits, F32)

    top_bit = 29

    def search(i, thr):
        cand = thr | jnp.left_shift(jnp.int32(1), top_bit - i)
        return jnp.where(count(aff >= as_float(cand)) >= cap, cand, thr)

    thr = lax.fori_loop(0, top_bit + 1, search, jnp.zeros((N_EXPERTS, 1, 1), jnp.int32))
    above = aff >= as_float(thr + 1)
    tied = (aff >= as_float(thr)) & jnp.logical_not(above)
    need = cap - count(above)

    r = lax.broadcasted_iota(jnp.int32, (LANES, LANES), 0)
    c = lax.broadcasted_iota(jnp.int32, (LANES, LANES), 1)
    before = jnp.where(r < c, 1.0, 0.0).astype(BF16)
    ones = jnp.ones((LANES, LANES), BF16)
    rb = lax.broadcasted_iota(jnp.int32, (nb, nb), 0)
    cb = lax.broadcasted_iota(jnp.int32, (nb, nb), 1)
    blocks_before = jnp.where(cb < rb, 1.0, 0.0).astype(BF16)

    def excl_cumsum(mask):
        m2 = jnp.where(mask, 1.0, 0.0).astype(BF16).reshape(N_EXPERTS * nb, LANES)
        within = _mm(m2, before).reshape(N_EXPERTS, nb, LANES)
        tot = _mm(m2, ones).astype(BF16).reshape(N_EXPERTS, nb, LANES)
        for e in range(N_EXPERTS):
            boff[e] = _mm(blocks_before, tot[e])
        return within + boff[...]

    sel = above | (tied & (excl_cumsum(tied) < need))
    pos = excl_cumsum(sel)
    pos_ref[...] = jnp.where(sel, pos, -1.0e6).astype(jnp.int32)
    off_ref[...] = boff[...]


def _route(afft, cap):
    e, n = afft.shape
    nb = n // LANES
    shape = (e, nb, LANES)
    return pl.pallas_call(
        functools.partial(_route_body, cap=cap),
        out_shape=[jax.ShapeDtypeStruct(shape, jnp.int32), jax.ShapeDtypeStruct(shape, F32)],
        scratch_shapes=[pltpu.VMEM(shape, F32)],
        compiler_params=pltpu.CompilerParams(vmem_limit_bytes=VMEM_LIMIT),
        name="route",
    )(afft.reshape(shape))


def _tile_counts_small(off_ref, experts, i):
    small = None
    for e in experts:
        ok = off_ref[e, i + 1] - off_ref[e, i] <= SMALL_COUNT
        small = ok if small is None else jnp.logical_and(small, ok)
    return small


def _dispatch_body(off_ref, pos_ref, h_ref, xe_ref, acc, *, cap):
    g0 = pl.program_id(0) * DISPATCH_EXPERTS
    nb = pos_ref.shape[1]
    d = h_ref.shape[1]
    acc[:, 0:SUBLANES, :] = jnp.zeros((DISPATCH_EXPERTS, SUBLANES, d), F32)
    head_row = lax.broadcasted_iota(jnp.int32, (SUBLANES, d), 0)

    def scatter(i, win):
        xt = h_ref[pl.ds(pl.multiple_of(i * TOK_TILE, TOK_TILE), TOK_TILE), :]
        r = lax.broadcasted_iota(jnp.int32, (win, TOK_TILE), 0)
        starts = [off_ref[g0 + j, i] for j in range(DISPATCH_EXPERTS)]
        bases = [pl.multiple_of((s // SUBLANES) * SUBLANES, SUBLANES) for s in starts]
        onehots = [jnp.where(pos_ref[j, pl.ds(i, 1), :] - bases[j] == r, 1.0, 0.0) for j in range(DISPATCH_EXPERTS)]
        stacked = _mm(jnp.concatenate(onehots, axis=0).astype(BF16), xt)
        for j in range(DISPATCH_EXPERTS):
            start, base = starts[j], bases[j]
            rows = stacked[j * win:(j + 1) * win]
            head = pl.ds(base, SUBLANES)
            acc[j, head, :] = jnp.where(head_row >= start - base, rows[:SUBLANES], acc[j, head, :])
            acc[j, pl.ds(base + SUBLANES, win - SUBLANES), :] = rows[SUBLANES:]

    def body(i, carry):
        small = _tile_counts_small(off_ref, [g0 + j for j in range(DISPATCH_EXPERTS)], i)

        @pl.when(small)
        def _():
            scatter(i, SMALL_WIN)

        @pl.when(jnp.logical_not(small))
        def _():
            scatter(i, BIG_WIN)

        return carry

    lax.fori_loop(0, nb, body, 0)
    for j in range(DISPATCH_EXPERTS):
        xe_ref[j] = acc[j, 0:cap, :].astype(BF16)


def _dispatch(off, pos, h2, cap):
    n, d = h2.shape
    nb = n // TOK_TILE
    return pl.pallas_call(
        functools.partial(_dispatch_body, cap=cap),
        grid_spec=pltpu.PrefetchScalarGridSpec(
            num_scalar_prefetch=1,
            grid=(N_EXPERTS // DISPATCH_EXPERTS,),
            in_specs=[
                pl.BlockSpec((DISPATCH_EXPERTS, nb, LANES), lambda e, off: (e, 0, 0)),
                pl.BlockSpec((n, d), lambda e, off: (0, 0), pipeline_mode=pl.Buffered(1)),
            ],
            out_specs=pl.BlockSpec((DISPATCH_EXPERTS, cap, d), lambda e, off: (e, 0, 0)),
            scratch_shapes=[pltpu.VMEM((DISPATCH_EXPERTS, cap + BIG_WIN, d), F32)],
        ),
        out_shape=jax.ShapeDtypeStruct((N_EXPERTS, cap, d), BF16),
        compiler_params=_cparams("arbitrary"),
        name="dispatch",
    )(off, pos, h2)


def _ffn_body(xe_ref, w1_ref, w3_ref, w2_ref, ye_ref):
    xe = xe_ref[0]
    a = _mm(xe, w1_ref[0].astype(BF16))
    b = _mm(xe, w3_ref[0].astype(BF16))
    hid = (a * jax.nn.sigmoid(a) * b).astype(BF16)
    ye_ref[0] = _mm(hid, w2_ref[0].astype(BF16)).astype(BF16)


def _ffn(xe, w1, w3, w2):
    e, cap, d = xe.shape
    f = w1.shape[2]
    return pl.pallas_call(
        _ffn_body,
        grid=(e,),
        in_specs=[
            pl.BlockSpec((1, cap, d), lambda i: (i, 0, 0)),
            pl.BlockSpec((1, d, f), lambda i: (i, 0, 0)),
            pl.BlockSpec((1, d, f), lambda i: (i, 0, 0)),
            pl.BlockSpec((1, f, d), lambda i: (i, 0, 0)),
        ],
        out_specs=pl.BlockSpec((1, cap, d), lambda i: (i, 0, 0)),
        out_shape=jax.ShapeDtypeStruct((e, cap, d), BF16),
        compiler_params=_cparams("parallel"),
        name="ffn",
    )(xe, w1, w3, w2)


def _combine_body(off_ref, pos_ref, x_ref, afft_ref, g2_ref, ye_ref, o_ref, *, cap):
    i = pl.program_id(0)

    def gather(win):
        r = lax.broadcasted_iota(jnp.int32, (win, TOK_TILE), 0)
        hi, lo, wins = [], [], []
        for e in range(N_EXPERTS):
            start = jnp.minimum((off_ref[e, i] // SUBLANES) * SUBLANES, cap - win)
            start = pl.multiple_of(start, SUBLANES)
            gated = jnp.where(pos_ref[e] - start == r, afft_ref[e:e + 1, :], 0.0)
            g_hi, g_lo = _split(gated)
            hi.append(g_hi)
            lo.append(g_lo)
            wins.append(ye_ref[e, pl.ds(start, win), :])
        gates = jnp.concatenate([jnp.concatenate(hi, axis=0), jnp.concatenate(lo, axis=0)], axis=1)
        both = _mm(gates, jnp.concatenate(wins, axis=0), _TN)
        o_ref[...] = x_ref[...] + g2_ref[0] * (both[:TOK_TILE] + both[TOK_TILE:])

    small = None
    for e in range(N_EXPERTS):
        ok = off_ref[e, i + 1] - off_ref[e, i] <= COMBINE_SMALL_WIN - SUBLANES + 1
        small = ok if small is None else jnp.logical_and(small, ok)

    @pl.when(small)
    def _():
        gather(COMBINE_SMALL_WIN)

    @pl.when(jnp.logical_not(small))
    def _():
        gather(BIG_WIN)


def _combine(off, pos, x, afft, gate2, ye, rows_per_mod):
    n, d = x.shape
    nb = n // TOK_TILE
    per = rows_per_mod // TOK_TILE
    cap = ye.shape[1]
    return pl.pallas_call(
        functools.partial(_combine_body, cap=cap),
        grid_spec=pltpu.PrefetchScalarGridSpec(
            num_scalar_prefetch=1,
            grid=(nb,),
            in_specs=[
                pl.BlockSpec((N_EXPERTS, None, 1, LANES), lambda i, off: (0, i, 0, 0)),
                pl.BlockSpec((TOK_TILE, d), lambda i, off: (i, 0)),
                pl.BlockSpec((N_EXPERTS, TOK_TILE), lambda i, off: (0, i)),
                pl.BlockSpec((1, 1, d), lambda i, off: (i // per, 0, 0)),
                pl.BlockSpec(ye.shape, lambda i, off: (0, 0, 0), pipeline_mode=pl.Buffered(1)),
            ],
            out_specs=pl.BlockSpec((TOK_TILE, d), lambda i, off: (i, 0)),
        ),
        out_shape=jax.ShapeDtypeStruct((n, d), F32),
        compiler_params=_cparams("arbitrary"),
        name="combine",
    )(off, pos.reshape(N_EXPERTS, nb, 1, LANES), x, afft, gate2, ye)


def _block_diag(w):
    z = jnp.zeros_like(w[:, 0::2])
    top = jnp.concatenate([w[:, 0::2], z], axis=-1)
    bot = jnp.concatenate([z, w[:, 1::2]], axis=-1)
    return jnp.concatenate([top, bot], axis=-2)


def _rope_tables(n_tokens):
    rows = n_tokens // GRID_W
    row = jnp.repeat(jnp.arange(rows, dtype=F32), GRID_W)
    col = jnp.tile(jnp.arange(GRID_W, dtype=F32), rows)
    quarter = HEAD_DIM // 4
    inv_freq = ROPE_THETA ** (-jnp.arange(quarter, dtype=F32) / quarter)
    ang = jnp.concatenate([row[:, None] * inv_freq, col[:, None] * inv_freq], axis=-1)
    cos, sin = jnp.cos(ang), jnp.sin(ang)
    cos_l = jnp.tile(cos, (1, LANES // (HEAD_DIM // 2)))
    sin_l = jnp.tile(jnp.concatenate([-sin, sin], axis=-1), (1, LANES // HEAD_DIM))
    return cos_l, sin_l


def _moe(x_mid, h2, afft, gate2, w1, w3, w2, rows_per_mod):
    n = x_mid.shape[0]
    cap = max(1, EC_CAPACITY_FACTOR * n // N_EXPERTS)
    pos, off = _route(afft, cap)
    off_i = jnp.concatenate([off[:, :, 0].astype(jnp.int32), jnp.full((N_EXPERTS, 1), cap, jnp.int32)], axis=1)
    xe = _dispatch(off_i, pos, h2, cap)
    ye = _ffn(xe, w1, w3, w2)
    return _combine(off_i, pos, x_mid, afft, gate2, ye, rows_per_mod)


def kernel(x_prompt, x_sample, state_lru, cache_diff_k, cache_diff_v, state_delta, cache_gqa_k, cache_gqa_v,
           c, c_ctx, norm1_g, norm2_g, w_mod, b_mod, w_in_ab, lru_conv_w, lru_conv_b, lru_wa, lru_ba,
           lru_wi, lru_bi, lru_lam, diff_q_gain, diff_k_gain, diff_lam, diff_sub_gain, w_in_cd,
           delta_conv_w, delta_a_log, delta_dt_bias, delta_norm_g, gqa_q_gain, gqa_k_gain, w_out,
           router_w, exp_w1, exp_w3, exp_w2):
    d = D_MODEL
    dec_batch, dec_seq = x_sample.shape[:2]
    ctx_batch, ctx_seq = x_prompt.shape[:2]
    n_ctx_groups = c.shape[0]

    cs = jnp.concatenate([c_ctx[None, :], c, jnp.zeros((SUBLANES - 1 - n_ctx_groups, d), F32)], axis=0)
    mod = _mod_call(cs, w_mod, b_mod)

    r = jnp.arange(LANES)
    group_mean = jnp.where((r[:, None] // HEAD_DIM) == (r[None, :] // HEAD_DIM), 1.0 / HEAD_DIM, 0.0).astype(BF16)
    rope = _rope_tables(dec_seq)
    lane_pad = lambda v, at: jnp.zeros((1, LANES), F32).at[0, at:at + v.size].set(v.reshape(-1))

    def run_group(x3, mod_rows, ctx):
        batch, seq = x3.shape[:2]
        n = batch * seq
        x = x3.reshape(n, d)
        is_ctx = ctx is None
        news = []
        for l in range(DEPTH):
            li = l // 2
            m6 = mod[l, mod_rows[0]:mod_rows[1]].reshape(-1, 6, 1, d)
            shift1, scale1, gate1, shift2, scale2, gate2 = (m6[:, k] for k in range(6))
            g1 = norm1_g[l].reshape(1, d)
            if l % 2 == 0:
                z = _inproj(x, g1, scale1, shift1, w_in_ab[li].astype(BF16), seq if not is_ctx else n)
                h0 = jnp.zeros((batch, 2, LRU_WIDTH), F32) if is_ctx else state_lru[:, li]
                ya, lru_state = _lru(z, batch, lru_conv_w[li], lru_conv_b[li].reshape(1, -1),
                                     _block_diag(lru_wa[li]), lru_ba[li], _block_diag(lru_wi[li]), lru_bi[li],
                                     lru_lam[li], h0)
                gq = jnp.tile(diff_q_gain[li], 512 // HEAD_DIM).reshape(1, -1)
                gk = jnp.tile(diff_k_gain[li], 512 // HEAD_DIM).reshape(1, -1)
                v = z[:, 2048:2560]
                if is_ctx:
                    qn, kc, kb = _prep(z, 2, 3, 512, gq, gk, group_mean, None, seq)
                    kk = [kb.reshape(batch, seq, 512)]
                    vv = [v.astype(BF16).reshape(batch, seq, 512)]
                    news.append((lru_state, kc.reshape(batch, seq, DIFF_HEADS, 2, HEAD_DIM),
                                 v.reshape(batch, seq, DIFF_HEADS, 2 * HEAD_DIM)))
                else:
                    qn, kb = _prep(z, 2, 3, 512, gq, gk, group_mean, rope, seq)
                    past = cache_diff_k.shape[2]
                    kk = [cache_diff_k[:, li].reshape(batch, past, 512).astype(BF16), kb.reshape(batch, seq, 512)]
                    vv = [cache_diff_v[:, li].reshape(batch, past, 512).astype(BF16),
                          v.astype(BF16).reshape(batch, seq, 512)]
                lam_init = 0.8 - 0.6 * math.exp(-0.3 * l)
                yb = _attn(qn, kk, vv, diff_lam[li], diff_sub_gain[li].reshape(1, -1), batch, True, lam_init)
            else:
                w = w_in_cd[li]
                w_p = jnp.concatenate([w[:, :2048], w[:, 2064:2832], w[:, 2048:2064],
                                       jnp.zeros((d, IN_CD_PAD - 2832), F32)], axis=1).astype(BF16)
                z = _inproj(x, g1, scale1, shift1, w_p, seq if not is_ctx else n)
                s0 = (jnp.zeros((batch, 2, DELTA_HEADS, DELTA_DK, DELTA_DK), F32) if is_ctx
                      else state_delta[:, li])
                ya, delta_state = _delta(z, batch, delta_conv_w[li], lane_pad(delta_a_log[li], 8),
                                         lane_pad(delta_dt_bias[li], 8), delta_norm_g[li].reshape(1, -1), s0)
                gq = jnp.tile(gqa_q_gain[li], 512 // HEAD_DIM).reshape(1, -1)
                gk = jnp.tile(gqa_k_gain[li], LANES // HEAD_DIM).reshape(1, -1)
                v = z[:, 2688:2816]
                dup = lambda a: jnp.repeat(a.reshape(a.shape[0], a.shape[1], GQA_KV_HEADS, 1, HEAD_DIM), 2,
                                           axis=3).reshape(a.shape[0], a.shape[1], 2 * LANES)
                if is_ctx:
                    qn, kc, kb = _prep(z, 4, 20, LANES, gq, gk, group_mean, None, seq)
                    kk = [dup(kb.reshape(batch, seq, LANES))]
                    vv = [dup(v.astype(BF16).reshape(batch, seq, LANES))]
                    news.append((delta_state, kc.reshape(batch, seq, GQA_KV_HEADS, HEAD_DIM),
                                 v.reshape(batch, seq, GQA_KV_HEADS, HEAD_DIM)))
                else:
                    qn, kb = _prep(z, 4, 20, LANES, gq, gk, group_mean, rope, seq)
                    past = cache_gqa_k.shape[2]
                    kk = [dup(cache_gqa_k[:, li].reshape(batch, past, LANES).astype(BF16)),
                          dup(kb.reshape(batch, seq, LANES))]
                    vv = [dup(cache_gqa_v[:, li].reshape(batch, past, LANES).astype(BF16)),
                          dup(v.astype(BF16).reshape(batch, seq, LANES))]
                yb = _attn(qn, kk, vv, jnp.zeros((4, HEAD_DIM), F32), jnp.zeros((1, LANES), F32), batch, False, 0.0)
            wo = w_out[l].astype(BF16)
            x_mid, h2, afft = _outproj(x, ya, yb, wo[:512], wo[512:], gate1, norm2_g[l].reshape(1, d),
                                       scale2, shift2, router_w[l].T, seq if not is_ctx else n)
            x = _moe(x_mid, h2, afft, gate2, exp_w1[l], exp_w3[l], exp_w2[l], seq if not is_ctx else n)
        return x.reshape(batch, seq, d), news

    y_prompt, news = run_group(x_prompt, (0, 1), None)
    y_sample, _ = run_group(x_sample, (1, 1 + dec_batch), True)

    dtype = x_prompt.dtype
    even, odd = news[0::2], news[1::2]
    stack = lambda items, k: jnp.stack([it[k] for it in items], axis=1).astype(dtype)
    return (y_prompt, y_sample, stack(even, 0), stack(even, 1), stack(even, 2),
            stack(odd, 0), stack(odd, 1), stack(odd, 2))
```

```python
import functools
import math

import jax
import jax.numpy as jnp
from jax import lax
from jax.experimental import pallas as pl
from jax.experimental.pallas import tpu as pltpu

F32 = jnp.float32
BF16 = jnp.bfloat16

D_MODEL = 1024
DEPTH = 4
GRID_W = 64
HEAD_DIM = 64
ROPE_THETA = 10000.0
RMS_EPS = 1e-6
LRU_WIDTH = 512
LRU_BLOCKS = 8
LRU_C = 8.0
DIFF_HEADS = 4
DELTA_HEADS = 4
DELTA_DK = 128
GQA_HEADS = 8
GQA_KV_HEADS = 2
N_EXPERTS = 16
EC_CAPACITY_FACTOR = 2
EXPERT_FF = 512
IN_AB = 2560
IN_CD_PAD = 2944

LANES = 128
SUBLANES = 8
VMEM_LIMIT = 56 * 1024 * 1024

Q_SCALE = HEAD_DIM ** -0.5 * math.log2(math.e)
CHUNK = 128
TOK_TILE = 128
DISPATCH_EXPERTS = 4
SMALL_WIN = 48
SMALL_COUNT = SMALL_WIN - 2 * SUBLANES + 1
BIG_WIN = TOK_TILE + 2 * SUBLANES
COMBINE_SMALL_WIN = SMALL_WIN


def _cparams(*sem):
    return pltpu.CompilerParams(dimension_semantics=sem, vmem_limit_bytes=VMEM_LIMIT)


def _mm(a, b, dims=(((1,), (0,)), ((), ()))):
    return lax.dot_general(a, b, dims, preferred_element_type=F32)


_NT = (((1,), (1,)), ((), ()))
_TN = (((0,), (0,)), ((), ()))


def _dot1(a, b, dims=(((1,), (0,)), ((), ()))):
    return _mm(a.astype(BF16), b.astype(BF16), dims)


def _split(x):
    hi = x.astype(BF16)
    lo = (x - hi.astype(F32)).astype(BF16)
    return hi, lo


def _dot3(a, b, dims=(((1,), (0,)), ((), ()))):
    ah, al = _split(a)
    bh, bl = _split(b)
    return _mm(ah, bh, dims) + (_mm(ah, bl, dims) + _mm(al, bh, dims))


def _dot_exact_rhs(m_bf16, x):
    x1 = x.astype(BF16)
    r1 = x - x1.astype(F32)
    x2 = r1.astype(BF16)
    x3 = (r1 - x2.astype(F32)).astype(BF16)
    return _mm(m_bf16, x1) + (_mm(m_bf16, x2) + _mm(m_bf16, x3))


def _dot_exact_lhs(x, m_bf16):
    x1 = x.astype(BF16)
    r1 = x - x1.astype(F32)
    x2 = r1.astype(BF16)
    x3 = (r1 - x2.astype(F32)).astype(BF16)
    return _mm(x1, m_bf16) + (_mm(x2, m_bf16) + _mm(x3, m_bf16))


def _rms(x, g):
    return x * lax.rsqrt(jnp.mean(x * x, axis=-1, keepdims=True) + RMS_EPS) * g


def _mod_body(c_ref, w_ref, b_ref, o_ref):
    c = c_ref[...]
    a = c * jax.nn.sigmoid(c)
    o_ref[0] = _dot3(a, w_ref[0]) + b_ref[0]


def _mod_call(cs, w_mod, b_mod):
    rows, d = cs.shape
    width = w_mod.shape[2]
    tn = 512
    return pl.pallas_call(
        _mod_body,
        grid=(DEPTH, width // tn),
        in_specs=[
            pl.BlockSpec((rows, d), lambda l, j: (0, 0)),
            pl.BlockSpec((1, d, tn), lambda l, j: (l, 0, j)),
            pl.BlockSpec((1, 1, tn), lambda l, j: (l, 0, j)),
        ],
        out_specs=pl.BlockSpec((1, rows, tn), lambda l, j: (l, 0, j)),
        out_shape=jax.ShapeDtypeStruct((DEPTH, rows, width), F32),
        compiler_params=_cparams("parallel", "parallel"),
        name="mod",
    )(cs, w_mod, b_mod.reshape(DEPTH, 1, width))


def _inproj_body(x_ref, g_ref, sc_ref, sh_ref, w_ref, z_ref):
    h = _rms(x_ref[...], g_ref[...]) * (1.0 + sc_ref[0]) + sh_ref[0]
    z_ref[...] = _mm(h.astype(BF16), w_ref[...])


def _inproj(x, g, scale, shift, w, rows_per_mod):
    n, d = x.shape
    width = w.shape[1]
    tm = 256
    per = rows_per_mod // tm
    return pl.pallas_call(
        _inproj_body,
        grid=(n // tm,),
        in_specs=[
            pl.BlockSpec((tm, d), lambda i: (i, 0)),
            pl.BlockSpec((1, d), lambda i: (0, 0)),
            pl.BlockSpec((1, 1, d), lambda i: (i // per, 0, 0)),
            pl.BlockSpec((1, 1, d), lambda i: (i // per, 0, 0)),
            pl.BlockSpec((d, width), lambda i: (0, 0)),
        ],
        out_specs=pl.BlockSpec((tm, width), lambda i: (i, 0)),
        out_shape=jax.ShapeDtypeStruct((n, width), F32),
        compiler_params=_cparams("parallel"),
        name="inproj",
    )(x, g, scale, shift, w)


def _shift_rows(x, k, t_iota, seq):
    rows = x.shape[0]
    if k > 0:
        return jnp.where(t_iota >= k, pltpu.roll(x, k, 0), 0.0)
    return jnp.where(t_iota < seq + k, pltpu.roll(x, rows + k, 0), 0.0)


def _conv4(x, w_ref, t_iota, seq):
    return (
        w_ref[0:1, :] * _shift_rows(x, 2, t_iota, seq)
        + w_ref[1:2, :] * _shift_rows(x, 1, t_iota, seq)
        + w_ref[2:3, :] * x
        + w_ref[3:4, :] * _shift_rows(x, -1, t_iota, seq)
    )


def _tile_scan(a, b, row, reverse):
    for d in (1, 2, 4):
        if reverse:
            ok = row < SUBLANES - d
            a_sh = pltpu.roll(a, SUBLANES - d, 0)
            b_sh = pltpu.roll(b, SUBLANES - d, 0)
        else:
            ok = row >= d
            a_sh = pltpu.roll(a, d, 0)
            b_sh = pltpu.roll(b, d, 0)
        a_sh = jnp.where(ok, a_sh, 1.0)
        b_sh = jnp.where(ok, b_sh, 0.0)
        b = a * b_sh + b
        a = a * a_sh
    return a, b


def _lru_body(xa_ref, ga_ref, cw_ref, cb_ref, wa_ref, ba_ref, wi_ref, bi_ref, lam_ref, h0_ref,
              y_ref, st_ref, af, bf, ab, bb, hf, hb, *, seq):
    rows = xa_ref.shape[0]
    n_seq = rows // seq
    t_iota = lax.broadcasted_iota(jnp.int32, (rows, LANES), 0) % seq
    u = _conv4(xa_ref[...], cw_ref, t_iota, seq) + cb_ref[...]
    ub = u.astype(BF16)
    for d, (a_s, b_s) in enumerate(((af, bf), (ab, bb))):
        r = jax.nn.sigmoid(_mm(ub, wa_ref[d].astype(BF16)) + ba_ref[d:d + 1, :])
        i = jax.nn.sigmoid(_mm(ub, wi_ref[d].astype(BF16)) + bi_ref[d:d + 1, :])
        log_a = -LRU_C * r * jax.nn.softplus(-lam_ref[d:d + 1, :])
        a = jnp.exp(log_a)
        a_s[...] = a
        b_s[...] = jnp.sqrt(1.0 - a * a) * (i * u)

    n8 = seq // SUBLANES
    row = lax.broadcasted_iota(jnp.int32, (SUBLANES, LANES), 0)

    def step(i, carry):
        out = []
        for b in range(n_seq):
            h_f, h_b = carry[2 * b], carry[2 * b + 1]
            r0 = pl.multiple_of(b * seq + i * SUBLANES, SUBLANES)
            a_c, b_c = _tile_scan(af[pl.ds(r0, SUBLANES), :], bf[pl.ds(r0, SUBLANES), :], row, False)
            hf_t = a_c * h_f + b_c
            hf[pl.ds(r0, SUBLANES), :] = hf_t
            r1 = pl.multiple_of(b * seq + (n8 - 1 - i) * SUBLANES, SUBLANES)
            a_c, b_c = _tile_scan(ab[pl.ds(r1, SUBLANES), :], bb[pl.ds(r1, SUBLANES), :], row, True)
            hb_t = a_c * h_b + b_c
            hb[pl.ds(r1, SUBLANES), :] = hb_t
            out += [hf_t[SUBLANES - 1:SUBLANES, :], hb_t[0:1, :]]
        return tuple(out)

    init = tuple(h0_ref[b, d:d + 1, :] for b in range(n_seq) for d in range(2))
    final = lax.fori_loop(0, n8, step, init, unroll=2)
    y_ref[...] = ((hf[...] + hb[...]) * jax.nn.gelu(ga_ref[...])).astype(BF16)
    for b in range(n_seq):
        st_ref[b, 0:1, :] = final[2 * b]
        st_ref[b, 1:2, :] = final[2 * b + 1]


LRU_BLOCK_ROWS = 1024


def _lru(z, batch, conv_w, conv_b, wa_bd, ba, wi_bd, bi, lam, h0):
    n = z.shape[0]
    seq = n // batch
    nb = max(1, LRU_BLOCK_ROWS // seq)
    t = nb * seq
    nj = LRU_WIDTH // LANES
    return pl.pallas_call(
        functools.partial(_lru_body, seq=seq),
        grid=(batch // nb, nj),
        in_specs=[
            pl.BlockSpec((t, LANES), lambda b, j: (b, j)),
            pl.BlockSpec((t, LANES), lambda b, j: (b, nj + j)),
            pl.BlockSpec((4, LANES), lambda b, j: (0, j)),
            pl.BlockSpec((1, LANES), lambda b, j: (0, j)),
            pl.BlockSpec((2, None, LANES, LANES), lambda b, j: (0, j, 0, 0)),
            pl.BlockSpec((2, LANES), lambda b, j: (0, j)),
            pl.BlockSpec((2, None, LANES, LANES), lambda b, j: (0, j, 0, 0)),
            pl.BlockSpec((2, LANES), lambda b, j: (0, j)),
            pl.BlockSpec((2, LANES), lambda b, j: (0, j)),
            pl.BlockSpec((nb, 2, LANES), lambda b, j: (b, 0, j)),
        ],
        out_specs=[
            pl.BlockSpec((t, LANES), lambda b, j: (b, j)),
            pl.BlockSpec((nb, 2, LANES), lambda b, j: (b, 0, j)),
        ],
        out_shape=[
            jax.ShapeDtypeStruct((n, LRU_WIDTH), BF16),
            jax.ShapeDtypeStruct((batch, 2, LRU_WIDTH), F32),
        ],
        scratch_shapes=[pltpu.VMEM((t, LANES), F32)] * 6,
        compiler_params=_cparams("parallel", "parallel"),
        name="lru",
    )(z, z, conv_w, conv_b, wa_bd, ba, wi_bd, bi, lam, h0)


def _head_norm(x, g_ref, gm_ref):
    outs = []
    for j in range(x.shape[1] // LANES):
        xs = x[:, j * LANES:(j + 1) * LANES]
        sq = xs * xs
        hi, lo = _split(sq)
        ms = _mm(hi, gm_ref[...]) + _mm(lo, gm_ref[...])
        outs.append(xs * lax.rsqrt(ms + RMS_EPS) * g_ref[:, j * LANES:(j + 1) * LANES])
    return outs


def _rope(xs, cos, sin, first_half):
    swapped = jnp.where(first_half, pltpu.roll(xs, LANES - HEAD_DIM // 2, 1), pltpu.roll(xs, HEAD_DIM // 2, 1))
    return xs * cos + swapped * sin


def _prep_ctx_body(q_ref, k_ref, gq_ref, gk_ref, gm_ref, qo_ref, kc_ref, kb_ref):
    qs = _head_norm(q_ref[...], gq_ref, gm_ref)
    ks = _head_norm(k_ref[...], gk_ref, gm_ref)
    for j, xs in enumerate(qs):
        qo_ref[:, j * LANES:(j + 1) * LANES] = (xs * Q_SCALE).astype(BF16)
    for j, xs in enumerate(ks):
        kc_ref[:, j * LANES:(j + 1) * LANES] = xs
        kb_ref[:, j * LANES:(j + 1) * LANES] = xs.astype(BF16)


def _prep_rope_body(q_ref, k_ref, gq_ref, gk_ref, gm_ref, cos_ref, sin_ref, qo_ref, kb_ref):
    qs = _head_norm(q_ref[...], gq_ref, gm_ref)
    ks = _head_norm(k_ref[...], gk_ref, gm_ref)
    cos = cos_ref[...]
    sin = sin_ref[...]
    lane = lax.broadcasted_iota(jnp.int32, cos.shape, 1)
    first_half = (lane % HEAD_DIM) < HEAD_DIM // 2
    for j, xs in enumerate(qs):
        qo_ref[:, j * LANES:(j + 1) * LANES] = (_rope(xs, cos, sin, first_half) * Q_SCALE).astype(BF16)
    for j, xs in enumerate(ks):
        kb_ref[:, j * LANES:(j + 1) * LANES] = _rope(xs, cos, sin, first_half).astype(BF16)


def _prep(z, q_blk, k_blk, kw, gq, gk, gm, rope, seq):
    n = z.shape[0]
    tm = 256
    qw = 512
    in_specs = [
        pl.BlockSpec((tm, qw), lambda i: (i, q_blk)),
        pl.BlockSpec((tm, kw), lambda i: (i, k_blk)),
        pl.BlockSpec((1, qw), lambda i: (0, 0)),
        pl.BlockSpec((1, kw), lambda i: (0, 0)),
        pl.BlockSpec((LANES, LANES), lambda i: (0, 0)),
    ]
    q_out = (pl.BlockSpec((tm, qw), lambda i: (i, 0)), jax.ShapeDtypeStruct((n, qw), BF16))
    kb_out = (pl.BlockSpec((tm, kw), lambda i: (i, 0)), jax.ShapeDtypeStruct((n, kw), BF16))
    if rope is None:
        kc_out = (pl.BlockSpec((tm, kw), lambda i: (i, 0)), jax.ShapeDtypeStruct((n, kw), F32))
        outs = (q_out, kc_out, kb_out)
        body = _prep_ctx_body
        args = (z, z, gq, gk, gm)
    else:
        per = seq // tm
        in_specs += [pl.BlockSpec((tm, LANES), lambda i: (i % per, 0))] * 2
        outs = (q_out, kb_out)
        body = _prep_rope_body
        args = (z, z, gq, gk, gm, rope[0], rope[1])
    return pl.pallas_call(
        body,
        grid=(n // tm,),
        in_specs=in_specs,
        out_specs=[o[0] for o in outs],
        out_shape=[o[1] for o in outs],
        compiler_params=_cparams("parallel"),
        name="prep",
    )(*args)


def _attn_body(q_ref, *refs, n_parts, diff, lam_init, tk, share):
    k_refs, v_refs = refs[:n_parts], refs[n_parts:2 * n_parts]
    lam_ref, sg_ref, o_ref, s_scr = refs[2 * n_parts:]
    for j in range(q_ref.shape[1] // LANES):
        q_cols = slice(j * LANES, (j + 1) * LANES)
        kv_cols = slice((j // share) * LANES, (j // share + 1) * LANES)
        o_ref[:, q_cols] = _attn_pair(q_ref[:, q_cols], k_refs, v_refs, kv_cols, lam_ref, sg_ref, s_scr,
                                      diff=diff, lam_init=lam_init, tk=tk)


def _attn_pair(q, k_refs, v_refs, kv_cols, lam_ref, sg_ref, s_scr, *, diff, lam_init, tk):
    tq = q.shape[0]
    lane = lax.broadcasted_iota(jnp.int32, (tq, LANES), 1)
    zero = jnp.zeros_like(q)
    heads = (jnp.where(lane < HEAD_DIM, q, zero), jnp.where(lane >= HEAD_DIM, q, zero))
    blocks, col = [], 0
    for part, k_ref in enumerate(k_refs):
        for r0 in range(0, k_ref.shape[1], tk):
            blocks.append((part, r0, col))
            col += tk

    rms = [jnp.full((tq, LANES), -jnp.inf, F32)] * 2
    for part, r0, col in blocks:
        k = k_refs[part][0, r0:r0 + tk, kv_cols]
        for h, qh in enumerate(heads):
            s = _mm(qh, k, _NT)
            s_scr[h, :, col:col + tk] = s
            for c in range(tk // LANES):
                rms[h] = jnp.maximum(rms[h], s[:, c * LANES:(c + 1) * LANES])
    ms = [jnp.max(rm, axis=-1, keepdims=True) for rm in rms]

    ones = jnp.ones((tk, LANES), BF16)
    accs = [jnp.zeros((tq, 2 * LANES), F32)] * 2
    for part, r0, col in blocks:
        v1 = jnp.concatenate([v_refs[part][0, r0:r0 + tk, kv_cols], ones], axis=1)
        for h in range(2):
            accs[h] = accs[h] + _mm(jnp.exp2(s_scr[h, :, col:col + tk] - ms[h]).astype(BF16), v1)
    o0, o1 = (acc[:, :LANES] / acc[:, LANES:] for acc in accs)
    if diff:
        lv = lam_ref[...]
        lam = (jnp.exp(jnp.sum(lv[0:1] * lv[1:2], axis=-1, keepdims=True))
               - jnp.exp(jnp.sum(lv[2:3] * lv[3:4], axis=-1, keepdims=True)) + lam_init)
        y = _rms(o0 - lam * o1, sg_ref[...]) * (1.0 - lam_init)
    else:
        y = jnp.where(lane < HEAD_DIM, o0, o1)
    return y.astype(BF16)


ATTN_SHORT_KEYS = 1024


def _attn(qn, k_parts, v_parts, lam_p, sub_gain, batch, diff, lam_init):
    n, qw = qn.shape
    t = n // batch
    tq = 256
    nq = t // tq
    tkv = sum(k.shape[1] for k in k_parts)
    tk = min(512, min(k.shape[1] for k in k_parts))
    q_blocks = qw // LANES
    kv_blocks = k_parts[0].shape[2] // LANES
    share = q_blocks // kv_blocks
    per_step = q_blocks if tkv <= ATTN_SHORT_KEYS else 1
    kv_per_step = max(1, per_step // share)
    kv_map = (lambda b, j, i: (b, 0, j)) if per_step > 1 else (lambda b, j, i: (b, 0, j // share))
    kv_specs = [pl.BlockSpec((1, k.shape[1], kv_per_step * LANES), kv_map) for k in k_parts]
    return pl.pallas_call(
        functools.partial(_attn_body, n_parts=len(k_parts), diff=diff, lam_init=lam_init, tk=tk,
                          share=share if per_step > 1 else 1),
        grid=(batch, q_blocks // per_step, nq),
        in_specs=[pl.BlockSpec((tq, per_step * LANES), lambda b, j, i: (b * nq + i, j))] + kv_specs + kv_specs + [
            pl.BlockSpec((4, HEAD_DIM), lambda b, j, i: (0, 0)),
            pl.BlockSpec((1, LANES), lambda b, j, i: (0, 0)),
        ],
        out_specs=pl.BlockSpec((tq, per_step * LANES), lambda b, j, i: (b * nq + i, j)),
        out_shape=jax.ShapeDtypeStruct((n, qw), BF16),
        scratch_shapes=[pltpu.VMEM((2, tq, tkv), F32)],
        compiler_params=_cparams("parallel", "parallel", "parallel"),
        name="attn",
    )(qn, *k_parts, *v_parts, lam_p, sub_gain)


INV_BASE = 8


_BNN = (((2,), (1,)), ((0,), (0,)))
_BNT = (((2,), (2,)), ((0,), (0,)))


def _bdot3(a, b):
    ah, al = _split(a)
    bh, bl = _split(b)
    return _mm(ah, bh, _BNN) + (_mm(ah, bl, _BNN) + _mm(al, bh, _BNN))


def _unit_tri_inverse(a, eye, blocks):
    same_base, level_masks = blocks
    d = jnp.where(same_base, a, 0.0)
    p = eye - d
    q = _bdot3(d, d)
    p = p + _bdot3(p, q)
    q = _bdot3(q, q)
    x = p + _bdot3(p, q)
    for m in level_masks:
        x = x - _bdot3(_bdot3(x, jnp.where(m, a, 0.0)), x)
    return x


def _delta_prepare(qc, kc, vc, cc, n_fwd, consts):
    incl, strict, cum_m, eye, blocks = consts
    x1 = cc.astype(BF16)
    r1 = cc - x1.astype(F32)
    x2 = r1.astype(BF16)
    x3 = (r1 - x2.astype(F32)).astype(BF16)
    cum = _mm(cum_m, x1, _BNN) + (_mm(cum_m, x2, _BNN) + _mm(cum_m, x3, _BNN))
    cum_t = jnp.swapaxes(cum, 1, 2)
    g_col = cum[:, :, 2:3]
    g_row = cum_t[:, 2:3, :]
    beta = cc[:, :, 0:1]
    g_last = jnp.concatenate([g_row[:n_fwd, :, CHUNK - 1:CHUNK], g_row[n_fwd:, :, 0:1]], axis=0)
    decay = jnp.exp(jnp.where(incl > 0.0, g_col - g_row, -jnp.inf))
    kb = kc * beta
    kcb = kc.astype(BF16)
    a = jnp.where(strict > 0.0, _mm(kb.astype(BF16), kcb, _BNT) * decay, 0.0)
    inv = _unit_tri_inverse(a, eye, blocks)
    e_g = jnp.exp(g_col)
    sol = _bdot3(inv, jnp.concatenate([kb * e_g, vc * beta], axis=2))
    w = sol[:, :, :DELTA_DK]
    u = sol[:, :, DELTA_DK:]
    qk = jnp.where(incl > 0.0, _mm(qc.astype(BF16), kcb, _BNT) * decay, 0.0)
    k_tail = kc * jnp.exp(g_last - g_col)
    return (w.astype(BF16), u, qk.astype(BF16), (qc * e_g).astype(BF16), k_tail.astype(BF16), jnp.exp(g_last))


def _delta_apply(s, pre, g):
    w, u, qk, q_dec, k_tail, chunk_decay = (t[g] for t in pre)
    sb = s.astype(BF16)
    v_new = u - _mm(w, sb)
    vb = v_new.astype(BF16)
    o = _mm(q_dec, sb) + _mm(qk, vb)
    return o, s * chunk_decay + _mm(k_tail, vb, _TN)


def _delta_body(q_ref, k_ref, v_ref, gate_ref, lg_ref, cwq_ref, cwk_ref, cwv_ref, alog_ref, dtb_ref, ng_ref,
                s0_ref, o_ref, st_ref, qs, ks, vs, cs_f, cs_b, o_f, o_b, *, seq, group):
    rows = q_ref.shape[0]
    head = pl.program_id(1)
    t_iota = lax.broadcasted_iota(jnp.int32, (rows, LANES), 0) % seq
    lane = lax.broadcasted_iota(jnp.int32, (rows, LANES), 1)

    def conv_silu(ref, cw):
        x = _conv4(ref[...], cw, t_iota, seq)
        return x * jax.nn.sigmoid(x)

    q = conv_silu(q_ref, cwq_ref)
    k = conv_silu(k_ref, cwk_ref)
    qs[...] = q * lax.rsqrt(jnp.sum(q * q, axis=-1, keepdims=True) + 1e-6) * DELTA_DK ** -0.5
    ks[...] = k * lax.rsqrt(jnp.sum(k * k, axis=-1, keepdims=True) + 1e-6)
    vs[...] = conv_silu(v_ref, cwv_ref)
    lg = lg_ref[...]
    beta = jax.nn.sigmoid(lg)
    g_all = -jnp.exp(alog_ref[...]) * jax.nn.softplus(lg + dtb_ref[...])

    def col(arr, idx):
        return jnp.sum(jnp.where(lane == idx, arr, 0.0), axis=1, keepdims=True)

    cs_f[...] = jnp.where(lane == 0, col(beta, head), jnp.where(lane == 2, col(g_all, 2 * DELTA_HEADS + head), 0.0))
    cs_b[...] = jnp.where(lane == 0, col(beta, DELTA_HEADS + head),
                          jnp.where(lane == 2, col(g_all, 3 * DELTA_HEADS + head), 0.0))

    n_seq = rows // seq
    half = n_seq * group
    r = lax.broadcasted_iota(jnp.int32, (CHUNK, CHUNK), 0)
    c = lax.broadcasted_iota(jnp.int32, (CHUNK, CHUNK), 1)
    eye = jnp.where(r == c, 1.0, 0.0)
    level_masks = []
    size = INV_BASE
    while size < CHUNK:
        level_masks.append(((r // (2 * size)) == (c // (2 * size))) & ((r // size) != (c // size)))
        size *= 2
    blocks = ((r // INV_BASE) == (c // INV_BASE), level_masks)

    def per_direction(fwd, bwd):
        stack = lambda m: jnp.broadcast_to(jnp.where(m, 1.0, 0.0)[None], (half, CHUNK, CHUNK))
        return jnp.concatenate([stack(fwd), stack(bwd)], axis=0)

    incl = per_direction(r >= c, r <= c)
    consts = (incl, per_direction(r > c, r < c), incl.astype(BF16), eye, blocks)
    n = seq // CHUNK

    def chunk_rows(r0):
        return pl.ds(r0 if isinstance(r0, int) else pl.multiple_of(r0, CHUNK), CHUNK)

    def trip(i, states):
        rows_f = [[b * seq + (i * group + g) * CHUNK for g in range(group)] for b in range(n_seq)]
        rows_b = [[b * seq + (n - 1 - (i * group + g)) * CHUNK for g in range(group)] for b in range(n_seq)]
        chains = [(r0, cs_f) for rb in rows_f for r0 in rb] + [(r0, cs_b) for rb in rows_b for r0 in rb]
        load = lambda ref: jnp.stack([ref[chunk_rows(r0), :] for r0, _ in chains], axis=0)
        cc = jnp.stack([cref[chunk_rows(r0), :] for r0, cref in chains], axis=0)
        pre = _delta_prepare(load(qs), load(ks), load(vs), cc, half, consts)
        out = []
        for b in range(n_seq):
            s_f, s_b = states[2 * b], states[2 * b + 1]
            for g in range(group):
                o, s_f = _delta_apply(s_f, pre, b * group + g)
                o_f[chunk_rows(rows_f[b][g]), :] = o
                o, s_b = _delta_apply(s_b, pre, half + b * group + g)
                o_b[chunk_rows(rows_b[b][g]), :] = o
            out += [s_f, s_b]
        return tuple(out)

    init = tuple(s0_ref[b, d, 0] for b in range(n_seq) for d in range(2))
    final = trip(0, init) if n == group else lax.fori_loop(0, n // group, trip, init)
    for b in range(n_seq):
        st_ref[b, 0, 0] = final[2 * b]
        st_ref[b, 1, 0] = final[2 * b + 1]
    gate = gate_ref[...]
    o_ref[...] = (_rms(o_f[...] + o_b[...], ng_ref[...]) * (gate * jax.nn.sigmoid(gate))).astype(BF16)


DELTA_GROUP = 8
DELTA_BLOCK_ROWS = 1024


def _delta(z, batch, conv_w, alog_l, dtb_l, norm_g, s0):
    n = z.shape[0]
    t = n // batch
    h = DELTA_HEADS
    nb = max(1, DELTA_BLOCK_ROWS // t)
    rows = nb * t
    group = min(DELTA_GROUP, t // CHUNK)
    blk = lambda off: pl.BlockSpec((rows, LANES), lambda b, j: (b, off + j))
    cwb = lambda off: pl.BlockSpec((4, LANES), lambda b, j: (0, off + j))
    lg_blk = 22
    vec = pl.BlockSpec((1, LANES), lambda b, j: (0, 0))
    st_spec = pl.BlockSpec((nb, 2, 1, DELTA_DK, DELTA_DK), lambda b, j: (b, 0, j, 0, 0))
    return pl.pallas_call(
        functools.partial(_delta_body, seq=t, group=group),
        grid=(batch // nb, h),
        in_specs=[blk(0), blk(h), blk(2 * h), blk(3 * h), pl.BlockSpec((rows, LANES), lambda b, j: (b, lg_blk)),
                  cwb(0), cwb(h), cwb(2 * h), vec, vec, vec, st_spec],
        out_specs=[pl.BlockSpec((rows, LANES), lambda b, j: (b, j)), st_spec],
        out_shape=[jax.ShapeDtypeStruct((n, h * LANES), BF16),
                   jax.ShapeDtypeStruct((batch, 2, h, DELTA_DK, DELTA_DK), F32)],
        scratch_shapes=[pltpu.VMEM((rows, LANES), F32)] * 7,
        compiler_params=_cparams("parallel", "parallel"),
        name="delta",
    )(z, z, z, z, z, conv_w, conv_w, conv_w, alog_l, dtb_l, norm_g, s0)


def _outproj_body(x_ref, ya_ref, yb_ref, wa_ref, wb_ref, g1_ref, n2_ref, sc_ref, sh_ref, rwt_ref,
                  xo_ref, h2_ref, afft_ref):
    for c in range(x_ref.shape[0] // OUTPROJ_ROWS):
        rows = slice(c * OUTPROJ_ROWS, (c + 1) * OUTPROJ_ROWS)
        y = _mm(ya_ref[rows, :], wa_ref[...]) + _mm(yb_ref[rows, :], wb_ref[...])
        x = x_ref[rows, :] + g1_ref[0] * y
        xo_ref[rows, :] = x
        h = _rms(x, n2_ref[...]) * (1.0 + sc_ref[0]) + sh_ref[0]
        h2_ref[rows, :] = h.astype(BF16)
        lt = _dot3(rwt_ref[...], h, _NT)
        et = jnp.exp(lt - jnp.max(lt, axis=0, keepdims=True))
        afft_ref[:, rows] = et / jnp.sum(et, axis=0, keepdims=True)


OUTPROJ_ROWS = 256


def _outproj(x, ya, yb, wa, wb, gate1, n2, scale2, shift2, rwt, rows_per_mod):
    n, d = x.shape
    tm = 2 * OUTPROJ_ROWS
    per = rows_per_mod // tm
    half = ya.shape[1]
    row = lambda w: pl.BlockSpec((tm, w), lambda i: (i, 0))
    mod = pl.BlockSpec((1, 1, d), lambda i: (i // per, 0, 0))
    full = lambda a: pl.BlockSpec(a.shape, lambda i: (0,) * a.ndim)
    return pl.pallas_call(
        _outproj_body,
        grid=(n // tm,),
        in_specs=[row(d), row(half), row(half), full(wa), full(wb), mod, full(n2), mod, mod, full(rwt)],
        out_specs=[row(d), row(d), pl.BlockSpec((N_EXPERTS, tm), lambda i: (0, i))],
        out_shape=[jax.ShapeDtypeStruct((n, d), F32), jax.ShapeDtypeStruct((n, d), BF16),
                   jax.ShapeDtypeStruct((N_EXPERTS, n), F32)],
        compiler_params=_cparams("parallel"),
        name="outproj",
    )(x, ya, yb, wa, wb, gate1, n2, scale2, shift2, rwt)


def _route_body(aff_ref, pos_ref, off_ref, boff, *, cap):
    nb = aff_ref.shape[1]
    aff = aff_ref[...]

    def count(mask):
        c = jnp.sum(jnp.where(mask, 1.0, 0.0), axis=2, keepdims=True)
        return jnp.sum(c, axis=1, keepdims=True)

    def as_float(bits):
        return lax.bitcast_convert_type(bits, F32)

    top_bit = 29

    def search(i, thr):
        cand = thr | jnp.left_shift(jnp.int32(1), top_bit - i)
        return jnp.where(count(aff >= as_float(cand)) >= cap, cand, thr)

    thr = lax.fori_loop(0, top_bit + 1, search, jnp.zeros((N_EXPERTS, 1, 1), jnp.int32))
    above = aff >= as_float(thr + 1)
    tied = (aff >= as_float(thr)) & jnp.logical_not(above)
    need = cap - count(above)

    r = lax.broadcasted_iota(jnp.int32, (LANES, LANES), 0)
    c = lax.broadcasted_iota(jnp.int32, (LANES, LANES), 1)
    before = jnp.where(r < c, 1.0, 0.0).astype(BF16)
    ones = jnp.ones((LANES, LANES), BF16)
    rb = lax.broadcasted_iota(jnp.int32, (nb, nb), 0)
    cb = lax.broadcasted_iota(jnp.int32, (nb, nb), 1)
    blocks_before = jnp.where(cb < rb, 1.0, 0.0).astype(BF16)

    def excl_cumsum(mask):
        m2 = jnp.where(mask, 1.0, 0.0).astype(BF16).reshape(N_EXPERTS * nb, LANES)
        within = _mm(m2, before).reshape(N_EXPERTS, nb, LANES)
        tot = _mm(m2, ones).astype(BF16).reshape(N_EXPERTS, nb, LANES)
        for e in range(N_EXPERTS):
            boff[e] = _mm(blocks_before, tot[e])
        return within + boff[...]

    sel = above | (tied & (excl_cumsum(tied) < need))
    pos = excl_cumsum(sel)
    pos_ref[...] = jnp.where(sel, pos, -1.0e6).astype(jnp.int32)
    off_ref[...] = boff[...]


def _route(afft, cap):
    e, n = afft.shape
    nb = n // LANES
    shape = (e, nb, LANES)
    return pl.pallas_call(
        functools.partial(_route_body, cap=cap),
        out_shape=[jax.ShapeDtypeStruct(shape, jnp.int32), jax.ShapeDtypeStruct(shape, F32)],
        scratch_shapes=[pltpu.VMEM(shape, F32)],
        compiler_params=pltpu.CompilerParams(vmem_limit_bytes=VMEM_LIMIT),
        name="route",
    )(afft.reshape(shape))


def _tile_counts_small(off_ref, experts, i):
    small = None
    for e in experts:
        ok = off_ref[e, i + 1] - off_ref[e, i] <= SMALL_COUNT
        small = ok if small is None else jnp.logical_and(small, ok)
    return small


def _dispatch_body(off_ref, pos_ref, h_ref, xe_ref, acc, *, cap):
    g0 = pl.program_id(0) * DISPATCH_EXPERTS
    nb = pos_ref.shape[1]
    d = h_ref.shape[1]
    acc[:, 0:SUBLANES, :] = jnp.zeros((DISPATCH_EXPERTS, SUBLANES, d), F32)
    head_row = lax.broadcasted_iota(jnp.int32, (SUBLANES, d), 0)

    def scatter(i, win):
        xt = h_ref[pl.ds(pl.multiple_of(i * TOK_TILE, TOK_TILE), TOK_TILE), :]
        r = lax.broadcasted_iota(jnp.int32, (win, TOK_TILE), 0)
        starts = [off_ref[g0 + j, i] for j in range(DISPATCH_EXPERTS)]
        bases = [pl.multiple_of((s // SUBLANES) * SUBLANES, SUBLANES) for s in starts]
        onehots = [jnp.where(pos_ref[j, pl.ds(i, 1), :] - bases[j] == r, 1.0, 0.0) for j in range(DISPATCH_EXPERTS)]
        stacked = _mm(jnp.concatenate(onehots, axis=0).astype(BF16), xt)
        for j in range(DISPATCH_EXPERTS):
            start, base = starts[j], bases[j]
            rows = stacked[j * win:(j + 1) * win]
            head = pl.ds(base, SUBLANES)
            acc[j, head, :] = jnp.where(head_row >= start - base, rows[:SUBLANES], acc[j, head, :])
            acc[j, pl.ds(base + SUBLANES, win - SUBLANES), :] = rows[SUBLANES:]

    def body(i, carry):
        small = _tile_counts_small(off_ref, [g0 + j for j in range(DISPATCH_EXPERTS)], i)

        @pl.when(small)
        def _():
            scatter(i, SMALL_WIN)

        @pl.when(jnp.logical_not(small))
        def _():
            scatter(i, BIG_WIN)

        return carry

    lax.fori_loop(0, nb, body, 0)
    for j in range(DISPATCH_EXPERTS):
        xe_ref[j] = acc[j, 0:cap, :].astype(BF16)


def _dispatch(off, pos, h2, cap):
    n, d = h2.shape
    nb = n // TOK_TILE
    return pl.pallas_call(
        functools.partial(_dispatch_body, cap=cap),
        grid_spec=pltpu.PrefetchScalarGridSpec(
            num_scalar_prefetch=1,
            grid=(N_EXPERTS // DISPATCH_EXPERTS,),
            in_specs=[
                pl.BlockSpec((DISPATCH_EXPERTS, nb, LANES), lambda e, off: (e, 0, 0)),
                pl.BlockSpec((n, d), lambda e, off: (0, 0), pipeline_mode=pl.Buffered(1)),
            ],
            out_specs=pl.BlockSpec((DISPATCH_EXPERTS, cap, d), lambda e, off: (e, 0, 0)),
            scratch_shapes=[pltpu.VMEM((DISPATCH_EXPERTS, cap + BIG_WIN, d), F32)],
        ),
        out_shape=jax.ShapeDtypeStruct((N_EXPERTS, cap, d), BF16),
        compiler_params=_cparams("arbitrary"),
        name="dispatch",
    )(off, pos, h2)


def _ffn_body(xe_ref, w1_ref, w3_ref, w2_ref, ye_ref):
    xe = xe_ref[0]
    a = _mm(xe, w1_ref[0].astype(BF16))
    b = _mm(xe, w3_ref[0].astype(BF16))
    hid = (a * jax.nn.sigmoid(a) * b).astype(BF16)
    ye_ref[0] = _mm(hid, w2_ref[0].astype(BF16)).astype(BF16)


def _ffn(xe, w1, w3, w2, layer):
    e, cap, d = xe.shape
    f = w1.shape[3]
    return pl.pallas_call(
        _ffn_body,
        grid=(e,),
        in_specs=[
            pl.BlockSpec((1, cap, d), lambda i: (i, 0, 0)),
            pl.BlockSpec((None, 1, d, f), lambda i: (layer, i, 0, 0)),
            pl.BlockSpec((None, 1, d, f), lambda i: (layer, i, 0, 0)),
            pl.BlockSpec((None, 1, f, d), lambda i: (layer, i, 0, 0)),
        ],
        out_specs=pl.BlockSpec((1, cap, d), lambda i: (i, 0, 0)),
        out_shape=jax.ShapeDtypeStruct((e, cap, d), BF16),
        compiler_params=_cparams("parallel"),
        name="ffn",
    )(xe, w1, w3, w2)


def _combine_body(off_ref, pos_ref, x_ref, afft_ref, g2_ref, ye_ref, o_ref, *, cap):
    i = pl.program_id(0)

    def gather(win):
        r = lax.broadcasted_iota(jnp.int32, (win, TOK_TILE), 0)
        hi, lo, wins = [], [], []
        for e in range(N_EXPERTS):
            start = jnp.minimum((off_ref[e, i] // SUBLANES) * SUBLANES, cap - win)
            start = pl.multiple_of(start, SUBLANES)
            gated = jnp.where(pos_ref[e] - start == r, afft_ref[e:e + 1, :], 0.0)
            g_hi, g_lo = _split(gated)
            hi.append(g_hi)
            lo.append(g_lo)
            wins.append(ye_ref[e, pl.ds(start, win), :])
        gates = jnp.concatenate([jnp.concatenate(hi, axis=0), jnp.concatenate(lo, axis=0)], axis=1)
        both = _mm(gates, jnp.concatenate(wins, axis=0), _TN)
        o_ref[...] = x_ref[...] + g2_ref[0] * (both[:TOK_TILE] + both[TOK_TILE:])

    small = None
    for e in range(N_EXPERTS):
        ok = off_ref[e, i + 1] - off_ref[e, i] <= COMBINE_SMALL_WIN - SUBLANES + 1
        small = ok if small is None else jnp.logical_and(small, ok)

    @pl.when(small)
    def _():
        gather(COMBINE_SMALL_WIN)

    @pl.when(jnp.logical_not(small))
    def _():
        gather(BIG_WIN)


def _combine(off, pos, x, afft, gate2, ye, rows_per_mod):
    n, d = x.shape
    nb = n // TOK_TILE
    per = rows_per_mod // TOK_TILE
    cap = ye.shape[1]
    return pl.pallas_call(
        functools.partial(_combine_body, cap=cap),
        grid_spec=pltpu.PrefetchScalarGridSpec(
            num_scalar_prefetch=1,
            grid=(nb,),
            in_specs=[
                pl.BlockSpec((N_EXPERTS, None, 1, LANES), lambda i, off: (0, i, 0, 0)),
                pl.BlockSpec((TOK_TILE, d), lambda i, off: (i, 0)),
                pl.BlockSpec((N_EXPERTS, TOK_TILE), lambda i, off: (0, i)),
                pl.BlockSpec((1, 1, d), lambda i, off: (i // per, 0, 0)),
                pl.BlockSpec(ye.shape, lambda i, off: (0, 0, 0), pipeline_mode=pl.Buffered(1)),
            ],
            out_specs=pl.BlockSpec((TOK_TILE, d), lambda i, off: (i, 0)),
        ),
        out_shape=jax.ShapeDtypeStruct((n, d), F32),
        compiler_params=_cparams("arbitrary"),
        name="combine",
    )(off, pos.reshape(N_EXPERTS, nb, 1, LANES), x, afft, gate2, ye)


def _block_diag(w):
    z = jnp.zeros_like(w[:, 0::2])
    top = jnp.concatenate([w[:, 0::2], z], axis=-1)
    bot = jnp.concatenate([z, w[:, 1::2]], axis=-1)
    return jnp.concatenate([top, bot], axis=-2)


def _rope_tables(n_tokens):
    rows = n_tokens // GRID_W
    row = jnp.repeat(jnp.arange(rows, dtype=F32), GRID_W)
    col = jnp.tile(jnp.arange(GRID_W, dtype=F32), rows)
    quarter = HEAD_DIM // 4
    inv_freq = ROPE_THETA ** (-jnp.arange(quarter, dtype=F32) / quarter)
    ang = jnp.concatenate([row[:, None] * inv_freq, col[:, None] * inv_freq], axis=-1)
    cos, sin = jnp.cos(ang), jnp.sin(ang)
    cos_l = jnp.tile(cos, (1, LANES // (HEAD_DIM // 2)))
    sin_l = jnp.tile(jnp.concatenate([-sin, sin], axis=-1), (1, LANES // HEAD_DIM))
    return cos_l, sin_l


def _moe(x_mid, h2, afft, gate2, w1, w3, w2, layer, rows_per_mod):
    n = x_mid.shape[0]
    cap = max(1, EC_CAPACITY_FACTOR * n // N_EXPERTS)
    pos, off = _route(afft, cap)
    off_i = jnp.concatenate([off[:, :, 0].astype(jnp.int32), jnp.full((N_EXPERTS, 1), cap, jnp.int32)], axis=1)
    xe = _dispatch(off_i, pos, h2, cap)
    ye = _ffn(xe, w1, w3, w2, layer)
    return _combine(off_i, pos, x_mid, afft, gate2, ye, rows_per_mod)


def kernel(x_prompt, x_sample, state_lru, cache_diff_k, cache_diff_v, state_delta, cache_gqa_k, cache_gqa_v,
           c, c_ctx, norm1_g, norm2_g, w_mod, b_mod, w_in_ab, lru_conv_w, lru_conv_b, lru_wa, lru_ba,
           lru_wi, lru_bi, lru_lam, diff_q_gain, diff_k_gain, diff_lam, diff_sub_gain, w_in_cd,
           delta_conv_w, delta_a_log, delta_dt_bias, delta_norm_g, gqa_q_gain, gqa_k_gain, w_out,
           router_w, exp_w1, exp_w3, exp_w2):
    d = D_MODEL
    dec_batch, dec_seq = x_sample.shape[:2]
    ctx_batch, ctx_seq = x_prompt.shape[:2]
    n_ctx_groups = c.shape[0]

    cs = jnp.concatenate([c_ctx[None, :], c, jnp.zeros((SUBLANES - 1 - n_ctx_groups, d), F32)], axis=0)
    mod = _mod_call(cs, w_mod, b_mod)

    r = jnp.arange(LANES)
    group_mean = jnp.where((r[:, None] // HEAD_DIM) == (r[None, :] // HEAD_DIM), 1.0 / HEAD_DIM, 0.0).astype(BF16)
    rope = _rope_tables(dec_seq)
    lane_pad = lambda v, at: jnp.zeros((1, LANES), F32).at[0, at:at + v.size].set(v.reshape(-1))

    def run_group(x3, mod_rows, ctx):
        batch, seq = x3.shape[:2]
        n = batch * seq
        x = x3.reshape(n, d)
        is_ctx = ctx is None
        news = []
        for l in range(DEPTH):
            li = l // 2
            m6 = mod[l, mod_rows[0]:mod_rows[1]].reshape(-1, 6, 1, d)
            shift1, scale1, gate1, shift2, scale2, gate2 = (m6[:, k] for k in range(6))
            g1 = norm1_g[l].reshape(1, d)
            if l % 2 == 0:
                z = _inproj(x, g1, scale1, shift1, w_in_ab[li].astype(BF16), seq if not is_ctx else n)
                h0 = jnp.zeros((batch, 2, LRU_WIDTH), F32) if is_ctx else state_lru[:, li]
                ya, lru_state = _lru(z, batch, lru_conv_w[li], lru_conv_b[li].reshape(1, -1),
                                     _block_diag(lru_wa[li]), lru_ba[li], _block_diag(lru_wi[li]), lru_bi[li],
                                     lru_lam[li], h0)
                gq = jnp.tile(diff_q_gain[li], 512 // HEAD_DIM).reshape(1, -1)
                gk = jnp.tile(diff_k_gain[li], 512 // HEAD_DIM).reshape(1, -1)
                v = z[:, 2048:2560]
                if is_ctx:
                    qn, kc, kb = _prep(z, 2, 3, 512, gq, gk, group_mean, None, seq)
                    kk = [kb.reshape(batch, seq, 512)]
                    vv = [v.astype(BF16).reshape(batch, seq, 512)]
                    news.append((lru_state, kc.reshape(batch, seq, DIFF_HEADS, 2, HEAD_DIM),
                                 v.reshape(batch, seq, DIFF_HEADS, 2 * HEAD_DIM)))
                else:
                    qn, kb = _prep(z, 2, 3, 512, gq, gk, group_mean, rope, seq)
                    past = cache_diff_k.shape[2]
                    kk = [cache_diff_k[:, li].reshape(batch, past, 512).astype(BF16), kb.reshape(batch, seq, 512)]
                    vv = [cache_diff_v[:, li].reshape(batch, past, 512).astype(BF16),
                          v.astype(BF16).reshape(batch, seq, 512)]
                lam_init = 0.8 - 0.6 * math.exp(-0.3 * l)
                yb = _attn(qn, kk, vv, diff_lam[li], diff_sub_gain[li].reshape(1, -1), batch, True, lam_init)
            else:
                w = w_in_cd[li]
                w_p = jnp.concatenate([w[:, :2048], w[:, 2064:2832], w[:, 2048:2064],
                                       jnp.zeros((d, IN_CD_PAD - 2832), F32)], axis=1).astype(BF16)
                z = _inproj(x, g1, scale1, shift1, w_p, seq if not is_ctx else n)
                s0 = (jnp.zeros((batch, 2, DELTA_HEADS, DELTA_DK, DELTA_DK), F32) if is_ctx
                      else state_delta[:, li])
                ya, delta_state = _delta(z, batch, delta_conv_w[li], lane_pad(delta_a_log[li], 8),
                                         lane_pad(delta_dt_bias[li], 8), delta_norm_g[li].reshape(1, -1), s0)
                gq = jnp.tile(gqa_q_gain[li], 512 // HEAD_DIM).reshape(1, -1)
                gk = jnp.tile(gqa_k_gain[li], LANES // HEAD_DIM).reshape(1, -1)
                v = z[:, 2688:2816]
                dup = lambda a: jnp.repeat(a.reshape(a.shape[0], a.shape[1], GQA_KV_HEADS, 1, HEAD_DIM), 2,
                                           axis=3).reshape(a.shape[0], a.shape[1], 2 * LANES)
                if is_ctx:
                    qn, kc, kb = _prep(z, 4, 20, LANES, gq, gk, group_mean, None, seq)
                    kk = [dup(kb.reshape(batch, seq, LANES))]
                    vv = [dup(v.astype(BF16).reshape(batch, seq, LANES))]
                    news.append((delta_state, kc.reshape(batch, seq, GQA_KV_HEADS, HEAD_DIM),
                                 v.reshape(batch, seq, GQA_KV_HEADS, HEAD_DIM)))
                else:
                    qn, kb = _prep(z, 4, 20, LANES, gq, gk, group_mean, rope, seq)
                    past = cache_gqa_k.shape[2]
                    kk = [dup(cache_gqa_k[:, li].reshape(batch, past, LANES).astype(BF16)),
                          dup(kb.reshape(batch, seq, LANES))]
                    vv = [dup(cache_gqa_v[:, li].reshape(batch, past, LANES).astype(BF16)),
                          dup(v.astype(BF16).reshape(batch, seq, LANES))]
                yb = _attn(qn, kk, vv, jnp.zeros((4, HEAD_DIM), F32), jnp.zeros((1, LANES), F32), batch, False, 0.0)
            wo = w_out[l].astype(BF16)
            x_mid, h2, afft = _outproj(x, ya, yb, wo[:512], wo[512:], gate1, norm2_g[l].reshape(1, d),
                                       scale2, shift2, router_w[l].T, seq if not is_ctx else n)
            x = _moe(x_mid, h2, afft, gate2, exp_w1, exp_w3, exp_w2, l, seq if not is_ctx else n)
        return x.reshape(batch, seq, d), news

    y_prompt, news = run_group(x_prompt, (0, 1), None)
    y_sample, _ = run_group(x_sample, (1, 1 + dec_batch), True)

    dtype = x_prompt.dtype
    even, odd = news[0::2], news[1::2]
    stack = lambda items, k: jnp.stack([it[k] for it in items], axis=1).astype(dtype)
    return (y_prompt, y_sample, stack(even, 0), stack(even, 1), stack(even, 2),
            stack(odd, 0), stack(odd, 1), stack(odd, 2))
```

```python
import functools
import math

import jax
import jax.numpy as jnp
from jax import lax
from jax.experimental import pallas as pl
from jax.experimental.pallas import tpu as pltpu

F32 = jnp.float32
BF16 = jnp.bfloat16

D_MODEL = 1024
DEPTH = 4
GRID_W = 64
HEAD_DIM = 64
ROPE_THETA = 10000.0
RMS_EPS = 1e-6
LRU_WIDTH = 512
LRU_BLOCKS = 8
LRU_C = 8.0
DIFF_HEADS = 4
DELTA_HEADS = 4
DELTA_DK = 128
GQA_HEADS = 8
GQA_KV_HEADS = 2
N_EXPERTS = 16
EC_CAPACITY_FACTOR = 2
EXPERT_FF = 512
IN_AB = 2560
IN_CD_PAD = 2944

LANES = 128
SUBLANES = 8
VMEM_LIMIT = 56 * 1024 * 1024

Q_SCALE = HEAD_DIM ** -0.5 * math.log2(math.e)
CHUNK = 128
TOK_TILE = 128
DISPATCH_EXPERTS = 4
SMALL_WIN = 48
SMALL_COUNT = SMALL_WIN - 2 * SUBLANES + 1
BIG_WIN = TOK_TILE + 2 * SUBLANES
COMBINE_SMALL_WIN = SMALL_WIN


def _cparams(*sem):
    return pltpu.CompilerParams(dimension_semantics=sem, vmem_limit_bytes=VMEM_LIMIT)


def _mm(a, b, dims=(((1,), (0,)), ((), ()))):
    return lax.dot_general(a, b, dims, preferred_element_type=F32)


_NT = (((1,), (1,)), ((), ()))
_TN = (((0,), (0,)), ((), ()))


def _dot1(a, b, dims=(((1,), (0,)), ((), ()))):
    return _mm(a.astype(BF16), b.astype(BF16), dims)


def _split(x):
    hi = x.astype(BF16)
    lo = (x - hi.astype(F32)).astype(BF16)
    return hi, lo


def _dot3(a, b, dims=(((1,), (0,)), ((), ()))):
    ah, al = _split(a)
    bh, bl = _split(b)
    return _mm(ah, bh, dims) + (_mm(ah, bl, dims) + _mm(al, bh, dims))


def _dot_exact_rhs(m_bf16, x):
    x1 = x.astype(BF16)
    r1 = x - x1.astype(F32)
    x2 = r1.astype(BF16)
    x3 = (r1 - x2.astype(F32)).astype(BF16)
    return _mm(m_bf16, x1) + (_mm(m_bf16, x2) + _mm(m_bf16, x3))


def _dot_exact_lhs(x, m_bf16):
    x1 = x.astype(BF16)
    r1 = x - x1.astype(F32)
    x2 = r1.astype(BF16)
    x3 = (r1 - x2.astype(F32)).astype(BF16)
    return _mm(x1, m_bf16) + (_mm(x2, m_bf16) + _mm(x3, m_bf16))


def _rms(x, g):
    return x * lax.rsqrt(jnp.mean(x * x, axis=-1, keepdims=True) + RMS_EPS) * g


def _mod_body(c_ref, w_ref, b_ref, o_ref):
    c = c_ref[...]
    a = c * jax.nn.sigmoid(c)
    o_ref[0] = _dot3(a, w_ref[0]) + b_ref[0]


def _mod_call(cs, w_mod, b_mod):
    rows, d = cs.shape
    width = w_mod.shape[2]
    tn = 512
    return pl.pallas_call(
        _mod_body,
        grid=(DEPTH, width // tn),
        in_specs=[
            pl.BlockSpec((rows, d), lambda l, j: (0, 0)),
            pl.BlockSpec((1, d, tn), lambda l, j: (l, 0, j)),
            pl.BlockSpec((1, 1, tn), lambda l, j: (l, 0, j)),
        ],
        out_specs=pl.BlockSpec((1, rows, tn), lambda l, j: (l, 0, j)),
        out_shape=jax.ShapeDtypeStruct((DEPTH, rows, width), F32),
        compiler_params=_cparams("parallel", "parallel"),
        name="mod",
    )(cs, w_mod, b_mod.reshape(DEPTH, 1, width))


def _inproj_body(x_ref, g_ref, sc_ref, sh_ref, w_ref, z_ref):
    h = _rms(x_ref[...], g_ref[...]) * (1.0 + sc_ref[0]) + sh_ref[0]
    z_ref[...] = _mm(h.astype(BF16), w_ref[...])


def _inproj(x, g, scale, shift, w, rows_per_mod):
    n, d = x.shape
    width = w.shape[1]
    tm = 256
    per = rows_per_mod // tm
    return pl.pallas_call(
        _inproj_body,
        grid=(n // tm,),
        in_specs=[
            pl.BlockSpec((tm, d), lambda i: (i, 0)),
            pl.BlockSpec((1, d), lambda i: (0, 0)),
            pl.BlockSpec((1, 1, d), lambda i: (i // per, 0, 0)),
            pl.BlockSpec((1, 1, d), lambda i: (i // per, 0, 0)),
            pl.BlockSpec((d, width), lambda i: (0, 0)),
        ],
        out_specs=pl.BlockSpec((tm, width), lambda i: (i, 0)),
        out_shape=jax.ShapeDtypeStruct((n, width), F32),
        compiler_params=_cparams("parallel"),
        name="inproj",
    )(x, g, scale, shift, w)


def _shift_rows(x, k, t_iota, seq):
    rows = x.shape[0]
    if k > 0:
        return jnp.where(t_iota >= k, pltpu.roll(x, k, 0), 0.0)
    return jnp.where(t_iota < seq + k, pltpu.roll(x, rows + k, 0), 0.0)


def _conv4(x, w_ref, t_iota, seq):
    return (
        w_ref[0:1, :] * _shift_rows(x, 2, t_iota, seq)
        + w_ref[1:2, :] * _shift_rows(x, 1, t_iota, seq)
        + w_ref[2:3, :] * x
        + w_ref[3:4, :] * _shift_rows(x, -1, t_iota, seq)
    )


def _tile_scan(a, b, row, reverse):
    for d in (1, 2, 4):
        if reverse:
            ok = row < SUBLANES - d
            a_sh = pltpu.roll(a, SUBLANES - d, 0)
            b_sh = pltpu.roll(b, SUBLANES - d, 0)
        else:
            ok = row >= d
            a_sh = pltpu.roll(a, d, 0)
            b_sh = pltpu.roll(b, d, 0)
        a_sh = jnp.where(ok, a_sh, 1.0)
        b_sh = jnp.where(ok, b_sh, 0.0)
        b = a * b_sh + b
        a = a * a_sh
    return a, b


def _lru_body(xa_ref, ga_ref, cw_ref, cb_ref, wa_ref, ba_ref, wi_ref, bi_ref, lam_ref, h0_ref,
              y_ref, st_ref, af, bf, ab, bb, hf, hb, *, seq):
    rows = xa_ref.shape[0]
    n_seq = rows // seq
    t_iota = lax.broadcasted_iota(jnp.int32, (rows, LANES), 0) % seq
    u = _conv4(xa_ref[...], cw_ref, t_iota, seq) + cb_ref[...]
    ub = u.astype(BF16)
    for d, (a_s, b_s) in enumerate(((af, bf), (ab, bb))):
        r = jax.nn.sigmoid(_mm(ub, wa_ref[d].astype(BF16)) + ba_ref[d:d + 1, :])
        i = jax.nn.sigmoid(_mm(ub, wi_ref[d].astype(BF16)) + bi_ref[d:d + 1, :])
        log_a = -LRU_C * r * jax.nn.softplus(-lam_ref[d:d + 1, :])
        a = jnp.exp(log_a)
        a_s[...] = a
        b_s[...] = jnp.sqrt(1.0 - a * a) * (i * u)

    n8 = seq // SUBLANES
    row = lax.broadcasted_iota(jnp.int32, (SUBLANES, LANES), 0)

    def step(i, carry):
        out = []
        for b in range(n_seq):
            h_f, h_b = carry[2 * b], carry[2 * b + 1]
            r0 = pl.multiple_of(b * seq + i * SUBLANES, SUBLANES)
            a_c, b_c = _tile_scan(af[pl.ds(r0, SUBLANES), :], bf[pl.ds(r0, SUBLANES), :], row, False)
            hf_t = a_c * h_f + b_c
            hf[pl.ds(r0, SUBLANES), :] = hf_t
            r1 = pl.multiple_of(b * seq + (n8 - 1 - i) * SUBLANES, SUBLANES)
            a_c, b_c = _tile_scan(ab[pl.ds(r1, SUBLANES), :], bb[pl.ds(r1, SUBLANES), :], row, True)
            hb_t = a_c * h_b + b_c
            hb[pl.ds(r1, SUBLANES), :] = hb_t
            out += [hf_t[SUBLANES - 1:SUBLANES, :], hb_t[0:1, :]]
        return tuple(out)

    init = tuple(h0_ref[b, d:d + 1, :] for b in range(n_seq) for d in range(2))
    final = lax.fori_loop(0, n8, step, init, unroll=2)
    y_ref[...] = ((hf[...] + hb[...]) * jax.nn.gelu(ga_ref[...])).astype(BF16)
    for b in range(n_seq):
        st_ref[b, 0:1, :] = final[2 * b]
        st_ref[b, 1:2, :] = final[2 * b + 1]


LRU_BLOCK_ROWS = 1024


def _lru(z, batch, conv_w, conv_b, wa_bd, ba, wi_bd, bi, lam, h0):
    n = z.shape[0]
    seq = n // batch
    nb = max(1, LRU_BLOCK_ROWS // seq)
    t = nb * seq
    nj = LRU_WIDTH // LANES
    return pl.pallas_call(
        functools.partial(_lru_body, seq=seq),
        grid=(batch // nb, nj),
        in_specs=[
            pl.BlockSpec((t, LANES), lambda b, j: (b, j)),
            pl.BlockSpec((t, LANES), lambda b, j: (b, nj + j)),
            pl.BlockSpec((4, LANES), lambda b, j: (0, j)),
            pl.BlockSpec((1, LANES), lambda b, j: (0, j)),
            pl.BlockSpec((2, None, LANES, LANES), lambda b, j: (0, j, 0, 0)),
            pl.BlockSpec((2, LANES), lambda b, j: (0, j)),
            pl.BlockSpec((2, None, LANES, LANES), lambda b, j: (0, j, 0, 0)),
            pl.BlockSpec((2, LANES), lambda b, j: (0, j)),
            pl.BlockSpec((2, LANES), lambda b, j: (0, j)),
            pl.BlockSpec((nb, 2, LANES), lambda b, j: (b, 0, j)),
        ],
        out_specs=[
            pl.BlockSpec((t, LANES), lambda b, j: (b, j)),
            pl.BlockSpec((nb, 2, LANES), lambda b, j: (b, 0, j)),
        ],
        out_shape=[
            jax.ShapeDtypeStruct((n, LRU_WIDTH), BF16),
            jax.ShapeDtypeStruct((batch, 2, LRU_WIDTH), F32),
        ],
        scratch_shapes=[pltpu.VMEM((t, LANES), F32)] * 6,
        compiler_params=_cparams("parallel", "parallel"),
        name="lru",
    )(z, z, conv_w, conv_b, wa_bd, ba, wi_bd, bi, lam, h0)


def _head_norm(x, g_ref, gm_ref):
    outs = []
    for j in range(x.shape[1] // LANES):
        xs = x[:, j * LANES:(j + 1) * LANES]
        sq = xs * xs
        hi, lo = _split(sq)
        ms = _mm(hi, gm_ref[...]) + _mm(lo, gm_ref[...])
        outs.append(xs * lax.rsqrt(ms + RMS_EPS) * g_ref[:, j * LANES:(j + 1) * LANES])
    return outs


def _rope(xs, cos, sin, first_half):
    swapped = jnp.where(first_half, pltpu.roll(xs, LANES - HEAD_DIM // 2, 1), pltpu.roll(xs, HEAD_DIM // 2, 1))
    return xs * cos + swapped * sin


def _prep_ctx_body(q_ref, k_ref, gq_ref, gk_ref, gm_ref, qo_ref, kc_ref, kb_ref):
    qs = _head_norm(q_ref[...], gq_ref, gm_ref)
    ks = _head_norm(k_ref[...], gk_ref, gm_ref)
    for j, xs in enumerate(qs):
        qo_ref[:, j * LANES:(j + 1) * LANES] = (xs * Q_SCALE).astype(BF16)
    for j, xs in enumerate(ks):
        kc_ref[:, j * LANES:(j + 1) * LANES] = xs
        kb_ref[:, j * LANES:(j + 1) * LANES] = xs.astype(BF16)


def _prep_rope_body(q_ref, k_ref, gq_ref, gk_ref, gm_ref, cos_ref, sin_ref, qo_ref, kb_ref):
    qs = _head_norm(q_ref[...], gq_ref, gm_ref)
    ks = _head_norm(k_ref[...], gk_ref, gm_ref)
    cos = cos_ref[...]
    sin = sin_ref[...]
    lane = lax.broadcasted_iota(jnp.int32, cos.shape, 1)
    first_half = (lane % HEAD_DIM) < HEAD_DIM // 2
    for j, xs in enumerate(qs):
        qo_ref[:, j * LANES:(j + 1) * LANES] = (_rope(xs, cos, sin, first_half) * Q_SCALE).astype(BF16)
    for j, xs in enumerate(ks):
        kb_ref[:, j * LANES:(j + 1) * LANES] = _rope(xs, cos, sin, first_half).astype(BF16)


def _prep(z, q_blk, k_blk, kw, gq, gk, gm, rope, seq):
    n = z.shape[0]
    tm = 256
    qw = 512
    in_specs = [
        pl.BlockSpec((tm, qw), lambda i: (i, q_blk)),
        pl.BlockSpec((tm, kw), lambda i: (i, k_blk)),
        pl.BlockSpec((1, qw), lambda i: (0, 0)),
        pl.BlockSpec((1, kw), lambda i: (0, 0)),
        pl.BlockSpec((LANES, LANES), lambda i: (0, 0)),
    ]
    q_out = (pl.BlockSpec((tm, qw), lambda i: (i, 0)), jax.ShapeDtypeStruct((n, qw), BF16))
    kb_out = (pl.BlockSpec((tm, kw), lambda i: (i, 0)), jax.ShapeDtypeStruct((n, kw), BF16))
    if rope is None:
        kc_out = (pl.BlockSpec((tm, kw), lambda i: (i, 0)), jax.ShapeDtypeStruct((n, kw), F32))
        outs = (q_out, kc_out, kb_out)
        body = _prep_ctx_body
        args = (z, z, gq, gk, gm)
    else:
        per = seq // tm
        in_specs += [pl.BlockSpec((tm, LANES), lambda i: (i % per, 0))] * 2
        outs = (q_out, kb_out)
        body = _prep_rope_body
        args = (z, z, gq, gk, gm, rope[0], rope[1])
    return pl.pallas_call(
        body,
        grid=(n // tm,),
        in_specs=in_specs,
        out_specs=[o[0] for o in outs],
        out_shape=[o[1] for o in outs],
        compiler_params=_cparams("parallel"),
        name="prep",
    )(*args)


def _attn_body(q_ref, *refs, n_parts, diff, lam_init, tk, share):
    k_refs, v_refs = refs[:n_parts], refs[n_parts:2 * n_parts]
    lam_ref, sg_ref, o_ref, s_scr = refs[2 * n_parts:]
    for j in range(q_ref.shape[1] // LANES):
        q_cols = slice(j * LANES, (j + 1) * LANES)
        kv_cols = slice((j // share) * LANES, (j // share + 1) * LANES)
        o_ref[:, q_cols] = _attn_pair(q_ref[:, q_cols], k_refs, v_refs, kv_cols, lam_ref, sg_ref, s_scr,
                                      diff=diff, lam_init=lam_init, tk=tk)


def _attn_pair(q, k_refs, v_refs, kv_cols, lam_ref, sg_ref, s_scr, *, diff, lam_init, tk):
    tq = q.shape[0]
    lane = lax.broadcasted_iota(jnp.int32, (tq, LANES), 1)
    zero = jnp.zeros_like(q)
    heads = (jnp.where(lane < HEAD_DIM, q, zero), jnp.where(lane >= HEAD_DIM, q, zero))
    blocks, col = [], 0
    for part, k_ref in enumerate(k_refs):
        for r0 in range(0, k_ref.shape[1], tk):
            blocks.append((part, r0, col))
            col += tk

    rms = [jnp.full((tq, LANES), -jnp.inf, F32)] * 2
    for part, r0, col in blocks:
        k = k_refs[part][0, r0:r0 + tk, kv_cols]
        for h, qh in enumerate(heads):
            s = _mm(qh, k, _NT)
            s_scr[h, :, col:col + tk] = s
            for c in range(tk // LANES):
                rms[h] = jnp.maximum(rms[h], s[:, c * LANES:(c + 1) * LANES])
    ms = [jnp.max(rm, axis=-1, keepdims=True) for rm in rms]

    ones = jnp.ones((tk, LANES), BF16)
    accs = [jnp.zeros((tq, 2 * LANES), F32)] * 2
    for part, r0, col in blocks:
        v1 = jnp.concatenate([v_refs[part][0, r0:r0 + tk, kv_cols], ones], axis=1)
        for h in range(2):
            accs[h] = accs[h] + _mm(jnp.exp2(s_scr[h, :, col:col + tk] - ms[h]).astype(BF16), v1)
    o0, o1 = (acc[:, :LANES] / acc[:, LANES:] for acc in accs)
    if diff:
        lv = lam_ref[...]
        lam = (jnp.exp(jnp.sum(lv[0:1] * lv[1:2], axis=-1, keepdims=True))
               - jnp.exp(jnp.sum(lv[2:3] * lv[3:4], axis=-1, keepdims=True)) + lam_init)
        y = _rms(o0 - lam * o1, sg_ref[...]) * (1.0 - lam_init)
    else:
        y = jnp.where(lane < HEAD_DIM, o0, o1)
    return y.astype(BF16)


ATTN_SHORT_KEYS = 1024


def _attn(qn, k_parts, v_parts, lam_p, sub_gain, batch, diff, lam_init):
    n, qw = qn.shape
    t = n // batch
    tkv = sum(k.shape[1] for k in k_parts)
    tq = 256 if tkv <= ATTN_SHORT_KEYS else 512
    nq = t // tq
    tk = min(512, min(k.shape[1] for k in k_parts))
    q_blocks = qw // LANES
    kv_blocks = k_parts[0].shape[2] // LANES
    share = q_blocks // kv_blocks
    per_step = q_blocks if tkv <= ATTN_SHORT_KEYS else 1
    kv_per_step = max(1, per_step // share)
    kv_map = (lambda b, j, i: (b, 0, j)) if per_step > 1 else (lambda b, j, i: (b, 0, j // share))
    kv_specs = [pl.BlockSpec((1, k.shape[1], kv_per_step * LANES), kv_map) for k in k_parts]
    return pl.pallas_call(
        functools.partial(_attn_body, n_parts=len(k_parts), diff=diff, lam_init=lam_init, tk=tk,
                          share=share if per_step > 1 else 1),
        grid=(batch, q_blocks // per_step, nq),
        in_specs=[pl.BlockSpec((tq, per_step * LANES), lambda b, j, i: (b * nq + i, j))] + kv_specs + kv_specs + [
            pl.BlockSpec((4, HEAD_DIM), lambda b, j, i: (0, 0)),
            pl.BlockSpec((1, LANES), lambda b, j, i: (0, 0)),
        ],
        out_specs=pl.BlockSpec((tq, per_step * LANES), lambda b, j, i: (b * nq + i, j)),
        out_shape=jax.ShapeDtypeStruct((n, qw), BF16),
        scratch_shapes=[pltpu.VMEM((2, tq, tkv), F32)],
        compiler_params=_cparams("parallel", "parallel", "parallel"),
        name="attn",
    )(qn, *k_parts, *v_parts, lam_p, sub_gain)


INV_BASE = 8


_BNN = (((2,), (1,)), ((0,), (0,)))
_BNT = (((2,), (2,)), ((0,), (0,)))


def _bdot3(a, b):
    ah, al = _split(a)
    bh, bl = _split(b)
    return _mm(ah, bh, _BNN) + (_mm(ah, bl, _BNN) + _mm(al, bh, _BNN))


def _unit_tri_inverse(a, eye, blocks):
    same_base, level_masks = blocks
    d = jnp.where(same_base, a, 0.0)
    p = eye - d
    q = _bdot3(d, d)
    p = p + _bdot3(p, q)
    q = _bdot3(q, q)
    x = p + _bdot3(p, q)
    for m in level_masks:
        x = x - _bdot3(_bdot3(x, jnp.where(m, a, 0.0)), x)
    return x


def _delta_prepare(qc, kc, vc, cc, n_fwd, consts):
    incl, strict, cum_m, eye, blocks = consts
    x1 = cc.astype(BF16)
    r1 = cc - x1.astype(F32)
    x2 = r1.astype(BF16)
    x3 = (r1 - x2.astype(F32)).astype(BF16)
    cum = _mm(cum_m, x1, _BNN) + (_mm(cum_m, x2, _BNN) + _mm(cum_m, x3, _BNN))
    cum_t = jnp.swapaxes(cum, 1, 2)
    g_col = cum[:, :, 2:3]
    g_row = cum_t[:, 2:3, :]
    beta = cc[:, :, 0:1]
    g_last = jnp.concatenate([g_row[:n_fwd, :, CHUNK - 1:CHUNK], g_row[n_fwd:, :, 0:1]], axis=0)
    decay = jnp.exp(jnp.where(incl > 0.0, g_col - g_row, -jnp.inf))
    kb = kc * beta
    kcb = kc.astype(BF16)
    a = jnp.where(strict > 0.0, _mm(kb.astype(BF16), kcb, _BNT) * decay, 0.0)
    inv = _unit_tri_inverse(a, eye, blocks)
    e_g = jnp.exp(g_col)
    sol = _bdot3(inv, jnp.concatenate([kb * e_g, vc * beta], axis=2))
    w = sol[:, :, :DELTA_DK]
    u = sol[:, :, DELTA_DK:]
    qk = jnp.where(incl > 0.0, _mm(qc.astype(BF16), kcb, _BNT) * decay, 0.0)
    k_tail = kc * jnp.exp(g_last - g_col)
    return (w.astype(BF16), u, qk.astype(BF16), (qc * e_g).astype(BF16), k_tail.astype(BF16), jnp.exp(g_last))


def _delta_apply(s, pre, g):
    w, u, qk, q_dec, k_tail, chunk_decay = (t[g] for t in pre)
    sb = s.astype(BF16)
    v_new = u - _mm(w, sb)
    vb = v_new.astype(BF16)
    o = _mm(q_dec, sb) + _mm(qk, vb)
    return o, s * chunk_decay + _mm(k_tail, vb, _TN)


def _delta_body(q_ref, k_ref, v_ref, gate_ref, lg_ref, cwq_ref, cwk_ref, cwv_ref, alog_ref, dtb_ref, ng_ref,
                s0_ref, o_ref, st_ref, qs, ks, vs, cs_f, cs_b, o_f, o_b, *, seq, group):
    rows = q_ref.shape[0]
    head = pl.program_id(1)
    t_iota = lax.broadcasted_iota(jnp.int32, (rows, LANES), 0) % seq
    lane = lax.broadcasted_iota(jnp.int32, (rows, LANES), 1)

    def conv_silu(ref, cw):
        x = _conv4(ref[...], cw, t_iota, seq)
        return x * jax.nn.sigmoid(x)

    q = conv_silu(q_ref, cwq_ref)
    k = conv_silu(k_ref, cwk_ref)
    qs[...] = q * lax.rsqrt(jnp.sum(q * q, axis=-1, keepdims=True) + 1e-6) * DELTA_DK ** -0.5
    ks[...] = k * lax.rsqrt(jnp.sum(k * k, axis=-1, keepdims=True) + 1e-6)
    vs[...] = conv_silu(v_ref, cwv_ref)
    lg = lg_ref[...]
    beta = jax.nn.sigmoid(lg)
    g_all = -jnp.exp(alog_ref[...]) * jax.nn.softplus(lg + dtb_ref[...])

    def col(arr, idx):
        return jnp.sum(jnp.where(lane == idx, arr, 0.0), axis=1, keepdims=True)

    cs_f[...] = jnp.where(lane == 0, col(beta, head), jnp.where(lane == 2, col(g_all, 2 * DELTA_HEADS + head), 0.0))
    cs_b[...] = jnp.where(lane == 0, col(beta, DELTA_HEADS + head),
                          jnp.where(lane == 2, col(g_all, 3 * DELTA_HEADS + head), 0.0))

    n_seq = rows // seq
    half = n_seq * group
    r = lax.broadcasted_iota(jnp.int32, (CHUNK, CHUNK), 0)
    c = lax.broadcasted_iota(jnp.int32, (CHUNK, CHUNK), 1)
    eye = jnp.where(r == c, 1.0, 0.0)
    level_masks = []
    size = INV_BASE
    while size < CHUNK:
        level_masks.append(((r // (2 * size)) == (c // (2 * size))) & ((r // size) != (c // size)))
        size *= 2
    blocks = ((r // INV_BASE) == (c // INV_BASE), level_masks)

    def per_direction(fwd, bwd):
        stack = lambda m: jnp.broadcast_to(jnp.where(m, 1.0, 0.0)[None], (half, CHUNK, CHUNK))
        return jnp.concatenate([stack(fwd), stack(bwd)], axis=0)

    incl = per_direction(r >= c, r <= c)
    consts = (incl, per_direction(r > c, r < c), incl.astype(BF16), eye, blocks)
    n = seq // CHUNK

    def chunk_rows(r0):
        return pl.ds(r0 if isinstance(r0, int) else pl.multiple_of(r0, CHUNK), CHUNK)

    def trip(i, states):
        rows_f = [[b * seq + (i * group + g) * CHUNK for g in range(group)] for b in range(n_seq)]
        rows_b = [[b * seq + (n - 1 - (i * group + g)) * CHUNK for g in range(group)] for b in range(n_seq)]
        chains = [(r0, cs_f) for rb in rows_f for r0 in rb] + [(r0, cs_b) for rb in rows_b for r0 in rb]
        load = lambda ref: jnp.stack([ref[chunk_rows(r0), :] for r0, _ in chains], axis=0)
        cc = jnp.stack([cref[chunk_rows(r0), :] for r0, cref in chains], axis=0)
        pre = _delta_prepare(load(qs), load(ks), load(vs), cc, half, consts)
        out = []
        for b in range(n_seq):
            s_f, s_b = states[2 * b], states[2 * b + 1]
            for g in range(group):
                o, s_f = _delta_apply(s_f, pre, b * group + g)
                o_f[chunk_rows(rows_f[b][g]), :] = o
                o, s_b = _delta_apply(s_b, pre, half + b * group + g)
                o_b[chunk_rows(rows_b[b][g]), :] = o
            out += [s_f, s_b]
        return tuple(out)

    init = tuple(s0_ref[b, d, 0] for b in range(n_seq) for d in range(2))
    final = trip(0, init) if n == group else lax.fori_loop(0, n // group, trip, init)
    for b in range(n_seq):
        st_ref[b, 0, 0] = final[2 * b]
        st_ref[b, 1, 0] = final[2 * b + 1]
    gate = gate_ref[...]
    o_ref[...] = (_rms(o_f[...] + o_b[...], ng_ref[...]) * (gate * jax.nn.sigmoid(gate))).astype(BF16)


DELTA_GROUP = 8
DELTA_BLOCK_ROWS = 1024


def _delta(z, batch, conv_w, alog_l, dtb_l, norm_g, s0):
    n = z.shape[0]
    t = n // batch
    h = DELTA_HEADS
    nb = max(1, DELTA_BLOCK_ROWS // t)
    rows = nb * t
    group = min(DELTA_GROUP, t // CHUNK)
    blk = lambda off: pl.BlockSpec((rows, LANES), lambda b, j: (b, off + j))
    cwb = lambda off: pl.BlockSpec((4, LANES), lambda b, j: (0, off + j))
    lg_blk = 22
    vec = pl.BlockSpec((1, LANES), lambda b, j: (0, 0))
    st_spec = pl.BlockSpec((nb, 2, 1, DELTA_DK, DELTA_DK), lambda b, j: (b, 0, j, 0, 0))
    return pl.pallas_call(
        functools.partial(_delta_body, seq=t, group=group),
        grid=(batch // nb, h),
        in_specs=[blk(0), blk(h), blk(2 * h), blk(3 * h), pl.BlockSpec((rows, LANES), lambda b, j: (b, lg_blk)),
                  cwb(0), cwb(h), cwb(2 * h), vec, vec, vec, st_spec],
        out_specs=[pl.BlockSpec((rows, LANES), lambda b, j: (b, j)), st_spec],
        out_shape=[jax.ShapeDtypeStruct((n, h * LANES), BF16),
                   jax.ShapeDtypeStruct((batch, 2, h, DELTA_DK, DELTA_DK), F32)],
        scratch_shapes=[pltpu.VMEM((rows, LANES), F32)] * 7,
        compiler_params=_cparams("parallel", "parallel"),
        name="delta",
    )(z, z, z, z, z, conv_w, conv_w, conv_w, alog_l, dtb_l, norm_g, s0)


def _outproj_body(x_ref, ya_ref, yb_ref, wa_ref, wb_ref, g1_ref, n2_ref, sc_ref, sh_ref, rwt_ref,
                  xo_ref, h2_ref, afft_ref):
    for c in range(x_ref.shape[0] // OUTPROJ_ROWS):
        rows = slice(c * OUTPROJ_ROWS, (c + 1) * OUTPROJ_ROWS)
        y = _mm(ya_ref[rows, :], wa_ref[...]) + _mm(yb_ref[rows, :], wb_ref[...])
        x = x_ref[rows, :] + g1_ref[0] * y
        xo_ref[rows, :] = x
        h = _rms(x, n2_ref[...]) * (1.0 + sc_ref[0]) + sh_ref[0]
        h2_ref[rows, :] = h.astype(BF16)
        lt = _dot3(rwt_ref[...], h, _NT)
        et = jnp.exp(lt - jnp.max(lt, axis=0, keepdims=True))
        afft_ref[:, rows] = et / jnp.sum(et, axis=0, keepdims=True)


OUTPROJ_ROWS = 256


def _outproj(x, ya, yb, wa, wb, gate1, n2, scale2, shift2, rwt, rows_per_mod):
    n, d = x.shape
    tm = 2 * OUTPROJ_ROWS
    per = rows_per_mod // tm
    half = ya.shape[1]
    row = lambda w: pl.BlockSpec((tm, w), lambda i: (i, 0))
    mod = pl.BlockSpec((1, 1, d), lambda i: (i // per, 0, 0))
    full = lambda a: pl.BlockSpec(a.shape, lambda i: (0,) * a.ndim)
    return pl.pallas_call(
        _outproj_body,
        grid=(n // tm,),
        in_specs=[row(d), row(half), row(half), full(wa), full(wb), mod, full(n2), mod, mod, full(rwt)],
        out_specs=[row(d), row(d), pl.BlockSpec((N_EXPERTS, tm), lambda i: (0, i))],
        out_shape=[jax.ShapeDtypeStruct((n, d), F32), jax.ShapeDtypeStruct((n, d), BF16),
                   jax.ShapeDtypeStruct((N_EXPERTS, n), F32)],
        compiler_params=_cparams("parallel"),
        name="outproj",
    )(x, ya, yb, wa, wb, gate1, n2, scale2, shift2, rwt)


def _route_body(aff_ref, pos_ref, off_ref, boff, *, cap):
    nb = aff_ref.shape[1]
    aff = aff_ref[...]

    def count(mask):
        c = jnp.sum(jnp.where(mask, 1.0, 0.0), axis=2, keepdims=True)
        return jnp.sum(c, axis=1, keepdims=True)

    def as_float(bits):
        return lax.bitcast_convert_type(bits, F32)

    top_bit = 29

    def search(i, thr):
        cand = thr | jnp.left_shift(jnp.int32(1), top_bit - i)
        return jnp.where(count(aff >= as_float(cand)) >= cap, cand, thr)

    thr = lax.fori_loop(0, top_bit + 1, search, jnp.zeros((N_EXPERTS, 1, 1), jnp.int32))
    above = aff >= as_float(thr + 1)
    tied = (aff >= as_float(thr)) & jnp.logical_not(above)
    need = cap - count(above)

    r = lax.broadcasted_iota(jnp.int32, (LANES, LANES), 0)
    c = lax.broadcasted_iota(jnp.int32, (LANES, LANES), 1)
    before = jnp.where(r < c, 1.0, 0.0).astype(BF16)
    ones = jnp.ones((LANES, LANES), BF16)
    rb = lax.broadcasted_iota(jnp.int32, (nb, nb), 0)
    cb = lax.broadcasted_iota(jnp.int32, (nb, nb), 1)
    blocks_before = jnp.where(cb < rb, 1.0, 0.0).astype(BF16)

    def excl_cumsum(mask):
        m2 = jnp.where(mask, 1.0, 0.0).astype(BF16).reshape(N_EXPERTS * nb, LANES)
        within = _mm(m2, before).reshape(N_EXPERTS, nb, LANES)
        tot = _mm(m2, ones).astype(BF16).reshape(N_EXPERTS, nb, LANES)
        for e in range(N_EXPERTS):
            boff[e] = _mm(blocks_before, tot[e])
        return within + boff[...]

    sel = above | (tied & (excl_cumsum(tied) < need))
    pos = excl_cumsum(sel)
    pos_ref[...] = jnp.where(sel, pos, -1.0e6).astype(jnp.int32)
    off_ref[...] = boff[...]


def _route(afft, cap):
    e, n = afft.shape
    nb = n // LANES
    shape = (e, nb, LANES)
    return pl.pallas_call(
        functools.partial(_route_body, cap=cap),
        out_shape=[jax.ShapeDtypeStruct(shape, jnp.int32), jax.ShapeDtypeStruct(shape, F32)],
        scratch_shapes=[pltpu.VMEM(shape, F32)],
        compiler_params=pltpu.CompilerParams(vmem_limit_bytes=VMEM_LIMIT),
        name="route",
    )(afft.reshape(shape))


def _tile_counts_small(off_ref, experts, i):
    small = None
    for e in experts:
        ok = off_ref[e, i + 1] - off_ref[e, i] <= SMALL_COUNT
        small = ok if small is None else jnp.logical_and(small, ok)
    return small


def _dispatch_body(off_ref, pos_ref, h_ref, xe_ref, acc, *, cap):
    g0 = pl.program_id(0) * DISPATCH_EXPERTS
    nb = pos_ref.shape[1]
    d = h_ref.shape[1]
    acc[:, 0:SUBLANES, :] = jnp.zeros((DISPATCH_EXPERTS, SUBLANES, d), F32)
    head_row = lax.broadcasted_iota(jnp.int32, (SUBLANES, d), 0)

    def scatter(i, win):
        xt = h_ref[pl.ds(pl.multiple_of(i * TOK_TILE, TOK_TILE), TOK_TILE), :]
        r = lax.broadcasted_iota(jnp.int32, (win, TOK_TILE), 0)
        starts = [off_ref[g0 + j, i] for j in range(DISPATCH_EXPERTS)]
        bases = [pl.multiple_of((s // SUBLANES) * SUBLANES, SUBLANES) for s in starts]
        onehots = [jnp.where(pos_ref[j, pl.ds(i, 1), :] - bases[j] == r, 1.0, 0.0) for j in range(DISPATCH_EXPERTS)]
        stacked = _mm(jnp.concatenate(onehots, axis=0).astype(BF16), xt)
        for j in range(DISPATCH_EXPERTS):
            start, base = starts[j], bases[j]
            rows = stacked[j * win:(j + 1) * win]
            head = pl.ds(base, SUBLANES)
            acc[j, head, :] = jnp.where(head_row >= start - base, rows[:SUBLANES], acc[j, head, :])
            acc[j, pl.ds(base + SUBLANES, win - SUBLANES), :] = rows[SUBLANES:]

    def body(i, carry):
        small = _tile_counts_small(off_ref, [g0 + j for j in range(DISPATCH_EXPERTS)], i)

        @pl.when(small)
        def _():
            scatter(i, SMALL_WIN)

        @pl.when(jnp.logical_not(small))
        def _():
            scatter(i, BIG_WIN)

        return carry

    lax.fori_loop(0, nb, body, 0)
    for j in range(DISPATCH_EXPERTS):
        xe_ref[j] = acc[j, 0:cap, :].astype(BF16)


def _dispatch(off, pos, h2, cap):
    n, d = h2.shape
    nb = n // TOK_TILE
    return pl.pallas_call(
        functools.partial(_dispatch_body, cap=cap),
        grid_spec=pltpu.PrefetchScalarGridSpec(
            num_scalar_prefetch=1,
            grid=(N_EXPERTS // DISPATCH_EXPERTS,),
            in_specs=[
                pl.BlockSpec((DISPATCH_EXPERTS, nb, LANES), lambda e, off: (e, 0, 0)),
                pl.BlockSpec((n, d), lambda e, off: (0, 0), pipeline_mode=pl.Buffered(1)),
            ],
            out_specs=pl.BlockSpec((DISPATCH_EXPERTS, cap, d), lambda e, off: (e, 0, 0)),
            scratch_shapes=[pltpu.VMEM((DISPATCH_EXPERTS, cap + BIG_WIN, d), F32)],
        ),
        out_shape=jax.ShapeDtypeStruct((N_EXPERTS, cap, d), BF16),
        compiler_params=_cparams("arbitrary"),
        name="dispatch",
    )(off, pos, h2)


def _ffn_body(xe_ref, w1_ref, w3_ref, w2_ref, ye_ref):
    xe = xe_ref[0]
    a = _mm(xe, w1_ref[0].astype(BF16))
    b = _mm(xe, w3_ref[0].astype(BF16))
    hid = (a * jax.nn.sigmoid(a) * b).astype(BF16)
    ye_ref[0] = _mm(hid, w2_ref[0].astype(BF16)).astype(BF16)


def _ffn(xe, w1, w3, w2, layer):
    e, cap, d = xe.shape
    f = w1.shape[3]
    return pl.pallas_call(
        _ffn_body,
        grid=(e,),
        in_specs=[
            pl.BlockSpec((1, cap, d), lambda i: (i, 0, 0)),
            pl.BlockSpec((None, 1, d, f), lambda i: (layer, i, 0, 0)),
            pl.BlockSpec((None, 1, d, f), lambda i: (layer, i, 0, 0)),
            pl.BlockSpec((None, 1, f, d), lambda i: (layer, i, 0, 0)),
        ],
        out_specs=pl.BlockSpec((1, cap, d), lambda i: (i, 0, 0)),
        out_shape=jax.ShapeDtypeStruct((e, cap, d), BF16),
        compiler_params=_cparams("parallel"),
        name="ffn",
    )(xe, w1, w3, w2)


def _combine_body(off_ref, pos_ref, x_ref, afft_ref, g2_ref, ye_ref, o_ref, *, cap):
    i = pl.program_id(0)

    def gather(win):
        r = lax.broadcasted_iota(jnp.int32, (win, TOK_TILE), 0)
        hi, lo, wins = [], [], []
        for e in range(N_EXPERTS):
            start = jnp.minimum((off_ref[e, i] // SUBLANES) * SUBLANES, cap - win)
            start = pl.multiple_of(start, SUBLANES)
            gated = jnp.where(pos_ref[e] - start == r, afft_ref[e:e + 1, :], 0.0)
            g_hi, g_lo = _split(gated)
            hi.append(g_hi)
            lo.append(g_lo)
            wins.append(ye_ref[e, pl.ds(start, win), :])
        gates = jnp.concatenate([jnp.concatenate(hi, axis=0), jnp.concatenate(lo, axis=0)], axis=1)
        both = _mm(gates, jnp.concatenate(wins, axis=0), _TN)
        o_ref[...] = x_ref[...] + g2_ref[0] * (both[:TOK_TILE] + both[TOK_TILE:])

    small = None
    for e in range(N_EXPERTS):
        ok = off_ref[e, i + 1] - off_ref[e, i] <= COMBINE_SMALL_WIN - SUBLANES + 1
        small = ok if small is None else jnp.logical_and(small, ok)

    @pl.when(small)
    def _():
        gather(COMBINE_SMALL_WIN)

    @pl.when(jnp.logical_not(small))
    def _():
        gather(BIG_WIN)


def _combine(off, pos, x, afft, gate2, ye, rows_per_mod):
    n, d = x.shape
    nb = n // TOK_TILE
    per = rows_per_mod // TOK_TILE
    cap = ye.shape[1]
    return pl.pallas_call(
        functools.partial(_combine_body, cap=cap),
        grid_spec=pltpu.PrefetchScalarGridSpec(
            num_scalar_prefetch=1,
            grid=(nb,),
            in_specs=[
                pl.BlockSpec((N_EXPERTS, None, 1, LANES), lambda i, off: (0, i, 0, 0)),
                pl.BlockSpec((TOK_TILE, d), lambda i, off: (i, 0)),
                pl.BlockSpec((N_EXPERTS, TOK_TILE), lambda i, off: (0, i)),
                pl.BlockSpec((1, 1, d), lambda i, off: (i // per, 0, 0)),
                pl.BlockSpec(ye.shape, lambda i, off: (0, 0, 0), pipeline_mode=pl.Buffered(1)),
            ],
            out_specs=pl.BlockSpec((TOK_TILE, d), lambda i, off: (i, 0)),
        ),
        out_shape=jax.ShapeDtypeStruct((n, d), F32),
        compiler_params=_cparams("arbitrary"),
        name="combine",
    )(off, pos.reshape(N_EXPERTS, nb, 1, LANES), x, afft, gate2, ye)


def _block_diag(w):
    z = jnp.zeros_like(w[:, 0::2])
    top = jnp.concatenate([w[:, 0::2], z], axis=-1)
    bot = jnp.concatenate([z, w[:, 1::2]], axis=-1)
    return jnp.concatenate([top, bot], axis=-2)


def _rope_tables(n_tokens):
    rows = n_tokens // GRID_W
    row = jnp.repeat(jnp.arange(rows, dtype=F32), GRID_W)
    col = jnp.tile(jnp.arange(GRID_W, dtype=F32), rows)
    quarter = HEAD_DIM // 4
    inv_freq = ROPE_THETA ** (-jnp.arange(quarter, dtype=F32) / quarter)
    ang = jnp.concatenate([row[:, None] * inv_freq, col[:, None] * inv_freq], axis=-1)
    cos, sin = jnp.cos(ang), jnp.sin(ang)
    cos_l = jnp.tile(cos, (1, LANES // (HEAD_DIM // 2)))
    sin_l = jnp.tile(jnp.concatenate([-sin, sin], axis=-1), (1, LANES // HEAD_DIM))
    return cos_l, sin_l


def _moe(x_mid, h2, afft, gate2, w1, w3, w2, layer, rows_per_mod):
    n = x_mid.shape[0]
    cap = max(1, EC_CAPACITY_FACTOR * n // N_EXPERTS)
    pos, off = _route(afft, cap)
    off_i = jnp.concatenate([off[:, :, 0].astype(jnp.int32), jnp.full((N_EXPERTS, 1), cap, jnp.int32)], axis=1)
    xe = _dispatch(off_i, pos, h2, cap)
    ye = _ffn(xe, w1, w3, w2, layer)
    return _combine(off_i, pos, x_mid, afft, gate2, ye, rows_per_mod)


def kernel(x_prompt, x_sample, state_lru, cache_diff_k, cache_diff_v, state_delta, cache_gqa_k, cache_gqa_v,
           c, c_ctx, norm1_g, norm2_g, w_mod, b_mod, w_in_ab, lru_conv_w, lru_conv_b, lru_wa, lru_ba,
           lru_wi, lru_bi, lru_lam, diff_q_gain, diff_k_gain, diff_lam, diff_sub_gain, w_in_cd,
           delta_conv_w, delta_a_log, delta_dt_bias, delta_norm_g, gqa_q_gain, gqa_k_gain, w_out,
           router_w, exp_w1, exp_w3, exp_w2):
    d = D_MODEL
    dec_batch, dec_seq = x_sample.shape[:2]
    ctx_batch, ctx_seq = x_prompt.shape[:2]
    n_ctx_groups = c.shape[0]

    cs = jnp.concatenate([c_ctx[None, :], c, jnp.zeros((SUBLANES - 1 - n_ctx_groups, d), F32)], axis=0)
    mod = _mod_call(cs, w_mod, b_mod)

    r = jnp.arange(LANES)
    group_mean = jnp.where((r[:, None] // HEAD_DIM) == (r[None, :] // HEAD_DIM), 1.0 / HEAD_DIM, 0.0).astype(BF16)
    rope = _rope_tables(dec_seq)
    lane_pad = lambda v, at: jnp.zeros((1, LANES), F32).at[0, at:at + v.size].set(v.reshape(-1))

    def run_group(x3, mod_rows, ctx):
        batch, seq = x3.shape[:2]
        n = batch * seq
        x = x3.reshape(n, d)
        is_ctx = ctx is None
        news = []
        for l in range(DEPTH):
            li = l // 2
            m6 = mod[l, mod_rows[0]:mod_rows[1]].reshape(-1, 6, 1, d)
            shift1, scale1, gate1, shift2, scale2, gate2 = (m6[:, k] for k in range(6))
            g1 = norm1_g[l].reshape(1, d)
            if l % 2 == 0:
                z = _inproj(x, g1, scale1, shift1, w_in_ab[li].astype(BF16), seq if not is_ctx else n)
                h0 = jnp.zeros((batch, 2, LRU_WIDTH), F32) if is_ctx else state_lru[:, li]
                ya, lru_state = _lru(z, batch, lru_conv_w[li], lru_conv_b[li].reshape(1, -1),
                                     _block_diag(lru_wa[li]), lru_ba[li], _block_diag(lru_wi[li]), lru_bi[li],
                                     lru_lam[li], h0)
                gq = jnp.tile(diff_q_gain[li], 512 // HEAD_DIM).reshape(1, -1)
                gk = jnp.tile(diff_k_gain[li], 512 // HEAD_DIM).reshape(1, -1)
                v = z[:, 2048:2560]
                if is_ctx:
                    qn, kc, kb = _prep(z, 2, 3, 512, gq, gk, group_mean, None, seq)
                    kk = [kb.reshape(batch, seq, 512)]
                    vv = [v.astype(BF16).reshape(batch, seq, 512)]
                    news.append((lru_state, kc.reshape(batch, seq, DIFF_HEADS, 2, HEAD_DIM),
                                 v.reshape(batch, seq, DIFF_HEADS, 2 * HEAD_DIM)))
                else:
                    qn, kb = _prep(z, 2, 3, 512, gq, gk, group_mean, rope, seq)
                    past = cache_diff_k.shape[2]
                    kk = [cache_diff_k[:, li].reshape(batch, past, 512).astype(BF16), kb.reshape(batch, seq, 512)]
                    vv = [cache_diff_v[:, li].reshape(batch, past, 512).astype(BF16),
                          v.astype(BF16).reshape(batch, seq, 512)]
                lam_init = 0.8 - 0.6 * math.exp(-0.3 * l)
                yb = _attn(qn, kk, vv, diff_lam[li], diff_sub_gain[li].reshape(1, -1), batch, True, lam_init)
            else:
                w = w_in_cd[li]
                w_p = jnp.concatenate([w[:, :2048], w[:, 2064:2832], w[:, 2048:2064],
                                       jnp.zeros((d, IN_CD_PAD - 2832), F32)], axis=1).astype(BF16)
                z = _inproj(x, g1, scale1, shift1, w_p, seq if not is_ctx else n)
                s0 = (jnp.zeros((batch, 2, DELTA_HEADS, DELTA_DK, DELTA_DK), F32) if is_ctx
                      else state_delta[:, li])
                ya, delta_state = _delta(z, batch, delta_conv_w[li], lane_pad(delta_a_log[li], 8),
                                         lane_pad(delta_dt_bias[li], 8), delta_norm_g[li].reshape(1, -1), s0)
                gq = jnp.tile(gqa_q_gain[li], 512 // HEAD_DIM).reshape(1, -1)
                gk = jnp.tile(gqa_k_gain[li], LANES // HEAD_DIM).reshape(1, -1)
                v = z[:, 2688:2816]
                dup = lambda a: jnp.repeat(a.reshape(a.shape[0], a.shape[1], GQA_KV_HEADS, 1, HEAD_DIM), 2,
                                           axis=3).reshape(a.shape[0], a.shape[1], 2 * LANES)
                if is_ctx:
                    qn, kc, kb = _prep(z, 4, 20, LANES, gq, gk, group_mean, None, seq)
                    kk = [dup(kb.reshape(batch, seq, LANES))]
                    vv = [dup(v.astype(BF16).reshape(batch, seq, LANES))]
                    news.append((delta_state, kc.reshape(batch, seq, GQA_KV_HEADS, HEAD_DIM),
                                 v.reshape(batch, seq, GQA_KV_HEADS, HEAD_DIM)))
                else:
                    qn, kb = _prep(z, 4, 20, LANES, gq, gk, group_mean, rope, seq)
                    past = cache_gqa_k.shape[2]
                    kk = [dup(cache_gqa_k[:, li].reshape(batch, past, LANES).astype(BF16)),
                          dup(kb.reshape(batch, seq, LANES))]
                    vv = [dup(cache_gqa_v[:, li].reshape(batch, past, LANES).astype(BF16)),
                          dup(v.astype(BF16).reshape(batch, seq, LANES))]
                yb = _attn(qn, kk, vv, jnp.zeros((4, HEAD_DIM), F32), jnp.zeros((1, LANES), F32), batch, False, 0.0)
            wo = w_out[l].astype(BF16)
            x_mid, h2, afft = _outproj(x, ya, yb, wo[:512], wo[512:], gate1, norm2_g[l].reshape(1, d),
                                       scale2, shift2, router_w[l].T, seq if not is_ctx else n)
            x = _moe(x_mid, h2, afft, gate2, exp_w1, exp_w3, exp_w2, l, seq if not is_ctx else n)
        return x.reshape(batch, seq, d), news

    y_prompt, news = run_group(x_prompt, (0, 1), None)
    y_sample, _ = run_group(x_sample, (1, 1 + dec_batch), True)

    dtype = x_prompt.dtype
    even, odd = news[0::2], news[1::2]
    stack = lambda items, k: jnp.stack([it[k] for it in items], axis=1).astype(dtype)
    return (y_prompt, y_sample, stack(even, 0), stack(even, 1), stack(even, 2),
            stack(odd, 0), stack(odd, 1), stack(odd, 2))
```
